```python
import math
import jax, jax.numpy as jnp
from jax import lax
import numpy as np

D_MODEL = 2048
BATCH = 8
SEQ = 4096
DEPTH = 4
DEC_BATCH = 16
DEC_SEQ = 2048
PAST_LEN = 128

ATTN_WIDTH = D_MODEL // 2
CONV_WIDTH = D_MODEL - ATTN_WIDTH
HEAD_DIM = 64
V_DIM = 2 * HEAD_DIM
N_HEADS = ATTN_WIDTH // V_DIM
CONV_WIDTH_K = 31
CONV_HALF = CONV_WIDTH_K // 2
D_IN = 3 * ATTN_WIDTH + 2 * CONV_WIDTH
D_MIX = ATTN_WIDTH + CONV_WIDTH
N_EXPERTS = 64
TOP_K = 8
N_GROUPS = 8
TOPK_GROUPS = 4
D_EXPERT = 512
D_SHARED = 512
ROUTED_SCALE = 2.5
MOE_BLOCK = 128
Q_BLOCK = 128
LN_EPS = 1e-5
DEEPNORM_ALPHA = (2 * DEPTH) ** 0.25
DEEPNORM_BETA = (8 * DEPTH) ** -0.25

kernel_name = 'hybrid_diffattn_conformer_moe_encoder'


def layer_norm(x, g, b):
    xf = x.astype(jnp.float32)
    mu = jnp.mean(xf, axis=-1, keepdims=True)
    var = jnp.mean(jnp.square(xf - mu), axis=-1, keepdims=True)
    return ((xf - mu) * lax.rsqrt(var + LN_EPS) * g + b).astype(x.dtype)


def alibi_slopes():
    return jnp.exp2(-8.0 * jnp.arange(1, N_HEADS + 1, dtype=jnp.float32) / N_HEADS)


def diff_attention(q, k, v, lam, lam_init, subln_g):
    B, S = q.shape[0], q.shape[1]
    nb = S // Q_BLOCK
    slopes = alibi_slopes()
    pos = jnp.arange(S, dtype=jnp.float32)
    q_blocks = q.reshape(B, nb, Q_BLOCK, N_HEADS, 2, HEAD_DIM).swapaxes(0, 1)
    q_pos = pos.reshape(nb, Q_BLOCK)
    scale = HEAD_DIM ** -0.5

    def block(args):
        q_blk, qp = args
        s = jnp.einsum('bqhjd,bkhjd->bhjqk', q_blk, k).astype(jnp.float32) * scale
        bias = -slopes[:, None, None] * jnp.abs(qp[:, None] - pos[None, :])
        p = jax.nn.softmax(s + bias[None, :, None], axis=-1)
        p = p[:, :, 0] - lam * p[:, :, 1]
        o = jnp.einsum('bhqk,bkhe->bqhe', p.astype(v.dtype), v).astype(jnp.float32)
        o = o * lax.rsqrt(jnp.mean(o * o, axis=-1, keepdims=True) + LN_EPS) * subln_g * (1.0 - lam_init)
        return o.astype(v.dtype).reshape(B, Q_BLOCK, N_HEADS * V_DIM)

    out = lax.map(block, (q_blocks, q_pos))
    return out.swapaxes(0, 1).reshape(B, S, N_HEADS * V_DIM)


def conformer_conv(a, gate, conv_w, conv_b, ln_g, ln_b):
    u = a * jax.nn.sigmoid(gate)
    y = lax.conv_general_dilated(u, conv_w[:, None, :].astype(u.dtype), window_strides=(1,),
                                 padding=[(CONV_HALF, CONV_HALF)],
                                 dimension_numbers=('NWC', 'WIO', 'NWC'),
                                 feature_group_count=CONV_WIDTH) + conv_b
    return jax.nn.silu(layer_norm(y, ln_g, ln_b))


def swiglu(x, w_gu, w_d):
    f = w_d.shape[0]
    gu = x @ w_gu
    return (jax.nn.silu(gu[..., :f]) * gu[..., f:]) @ w_d


def routed_experts(hf, top_e, gates, w_gate_up, w_down):
    T, D = hf.shape
    TK = T * TOP_K
    flat_e = top_e.reshape(TK)
    flat_w = gates.reshape(TK)
    order = jnp.argsort(flat_e)
    sorted_e = flat_e[order]
    tok = (order // TOP_K).astype(jnp.int32)
    counts = jnp.bincount(flat_e, length=N_EXPERTS)
    starts = jnp.cumsum(counts) - counts
    padded = (counts + MOE_BLOCK - 1) // MOE_BLOCK * MOE_BLOCK
    pad_ends = jnp.cumsum(padded)
    pad_starts = pad_ends - padded
    dest = pad_starts[sorted_e] + jnp.arange(TK) - starts[sorted_e]
    n_rows = TK + N_EXPERTS * MOE_BLOCK
    n_blk = n_rows // MOE_BLOCK
    row_tok = jnp.full((n_rows,), T, jnp.int32).at[dest].set(tok)
    row_w = jnp.zeros((n_rows,), jnp.float32).at[dest].set(flat_w[order])
    blk_e = jnp.minimum(jnp.searchsorted(pad_ends, jnp.arange(n_blk) * MOE_BLOCK, side='right'), N_EXPERTS - 1)
    h_pad = jnp.concatenate([hf, jnp.zeros((1, D), hf.dtype)], axis=0)

    def block(args):
        idx, g, e = args
        y = swiglu(h_pad[idx], w_gate_up[e], w_down[e])
        return (y * g[:, None]).astype(hf.dtype)

    y = lax.map(block, (row_tok.reshape(n_blk, MOE_BLOCK), row_w.reshape(n_blk, MOE_BLOCK), blk_e))
    return jax.ops.segment_sum(y.reshape(n_rows, D), row_tok, num_segments=T + 1)[:T]


def moe(h, w_router, router_bias, w_gate_up, w_down, ws_gate_up, ws_down):
    B, S, D = h.shape
    T = B * S
    hf = h.reshape(T, D)
    scores = jax.nn.sigmoid((hf @ w_router).astype(jnp.float32))
    sel = scores + router_bias.astype(jnp.float32)
    grp = sel.reshape(T, N_GROUPS, N_EXPERTS // N_GROUPS)
    grp_score = jnp.sum(lax.top_k(grp, 2)[0], axis=-1)
    _, top_g = lax.top_k(grp_score, TOPK_GROUPS)
    gmask = jnp.any(top_g[..., None] == jnp.arange(N_GROUPS)[None, None, :], axis=1)
    emask = jnp.repeat(gmask, N_EXPERTS // N_GROUPS, axis=1)
    _, top_e = lax.top_k(jnp.where(emask, sel, -jnp.inf), TOP_K)
    w = jnp.take_along_axis(scores, top_e, axis=1)
    w = w / jnp.sum(w, axis=-1, keepdims=True) * ROUTED_SCALE
    routed = routed_experts(hf, top_e, w, w_gate_up, w_down)
    shared = swiglu(hf, ws_gate_up, ws_down)
    return (routed + shared).reshape(B, S, D)


def encoder_layer(x, c, l, w_ada, b_ada, w_in, b_in, lq1, lk1, lq2, lk2, subln_g,
                  conv_w, conv_b, conv_ln_g, conv_ln_b, w_out, b_out, ln1_g, ln1_b,
                  w_router, router_bias, w_gate_up, w_down, ws_gate_up, ws_down, ln2_g, ln2_b):
    B, S, _ = x.shape
    mod = (jax.nn.silu(c) @ w_ada + b_ada)[:, None, :]
    sh_a, sc_a, g_a, sh_f, sc_f, g_f = jnp.split(mod, 6, axis=-1)
    h = x * (1.0 + sc_a) + sh_a
    proj = h @ w_in + b_in
    aw, cw = ATTN_WIDTH, CONV_WIDTH
    q = proj[..., :aw].reshape(B, S, N_HEADS, 2, HEAD_DIM)
    k = proj[..., aw:2 * aw].reshape(B, S, N_HEADS, 2, HEAD_DIM)
    v = proj[..., 2 * aw:3 * aw].reshape(B, S, N_HEADS, V_DIM)
    ca = proj[..., 3 * aw:3 * aw + cw]
    cg = proj[..., 3 * aw + cw:]
    lam_init = 0.8 - 0.6 * math.exp(-0.3 * l)
    lam = (jnp.exp(jnp.sum(lq1.astype(jnp.float32) * lk1.astype(jnp.float32)))
           - jnp.exp(jnp.sum(lq2.astype(jnp.float32) * lk2.astype(jnp.float32))) + lam_init)
    attn = diff_attention(q, k, v, lam, lam_init, subln_g)
    conv = conformer_conv(ca, cg, conv_w, conv_b, conv_ln_g, conv_ln_b)
    mix = jnp.concatenate([attn, conv.astype(attn.dtype)], axis=-1) @ w_out + b_out
    x = layer_norm(DEEPNORM_ALPHA * x + g_a * mix, ln1_g, ln1_b)
    h = x * (1.0 + sc_f) + sh_f
    ff = moe(h, w_router, router_bias, w_gate_up, w_down, ws_gate_up, ws_down)
    return layer_norm(DEEPNORM_ALPHA * x + g_f * ff, ln2_g, ln2_b)


def setup_inputs(seed: int = 0) -> dict:
    key = jax.random.key(seed)
    ks = iter(jax.random.split(key, 31))

    def nrm(shape, scale):
        return jax.random.normal(next(ks), shape, jnp.float32) * scale

    D, L, E = D_MODEL, DEPTH, N_EXPERTS
    return {
        'x_prompt': nrm((BATCH, SEQ, D), 1.0),
        'x_sample': nrm((DEC_BATCH, DEC_SEQ, D), 1.0),
        'c_prompt': nrm((BATCH, D), 1.0),
        'c_sample': nrm((DEC_BATCH, D), 1.0),
        'emb_ln_g': 1.0 + nrm((D,), 0.02),
        'emb_ln_b': nrm((D,), 0.02),
        'w_ada': nrm((L, D, 6 * D), 0.5 * D ** -0.5),
        'b_ada': nrm((L, 6 * D), 0.02),
        'w_in': nrm((L, D, D_IN), D ** -0.5),
        'b_in': nrm((L, D_IN), 0.02),
        'lambda_q1': nrm((L, HEAD_DIM), 0.1),
        'lambda_k1': nrm((L, HEAD_DIM), 0.1),
        'lambda_q2': nrm((L, HEAD_DIM), 0.1),
        'lambda_k2': nrm((L, HEAD_DIM), 0.1),
        'attn_subln_g': 1.0 + nrm((L, V_DIM), 0.02),
        'conv_w': nrm((L, CONV_WIDTH_K, CONV_WIDTH), CONV_WIDTH_K ** -0.5),
        'conv_b': nrm((L, CONV_WIDTH), 0.02),
        'conv_ln_g': 1.0 + nrm((L, CONV_WIDTH), 0.02),
        'conv_ln_b': nrm((L, CONV_WIDTH), 0.02),
        'w_out': nrm((L, D_MIX, D), DEEPNORM_BETA * D_MIX ** -0.5),
        'b_out': nrm((L, D), 0.02),
        'ln1_g': 1.0 + nrm((L, D), 0.02),
        'ln1_b': nrm((L, D), 0.02),
        'w_router': nrm((L, D, E), D ** -0.5),
        'router_bias': nrm((L, E), 0.01),
        'w_gate_up': nrm((L, E, D, 2 * D_EXPERT), D ** -0.5),
        'w_down': nrm((L, E, D_EXPERT, D), DEEPNORM_BETA * D_EXPERT ** -0.5),
        'ws_gate_up': nrm((L, D, 2 * D_SHARED), D ** -0.5),
        'ws_down': nrm((L, D_SHARED, D), DEEPNORM_BETA * D_SHARED ** -0.5),
        'ln2_g': 1.0 + nrm((L, D), 0.02),
        'ln2_b': nrm((L, D), 0.02),
    }


def reference(x_prompt, x_sample, c_prompt, c_sample, emb_ln_g, emb_ln_b, w_ada, b_ada,
              w_in, b_in, lambda_q1, lambda_k1, lambda_q2, lambda_k2, attn_subln_g,
              conv_w, conv_b, conv_ln_g, conv_ln_b, w_out, b_out, ln1_g, ln1_b,
              w_router, router_bias, w_gate_up, w_down, ws_gate_up, ws_down, ln2_g, ln2_b):
    def trunk(x, c):
        x = layer_norm(x, emb_ln_g, emb_ln_b)
        for l in range(DEPTH):
            x = encoder_layer(x, c, l, w_ada[l], b_ada[l], w_in[l], b_in[l],
                              lambda_q1[l], lambda_k1[l], lambda_q2[l], lambda_k2[l], attn_subln_g[l],
                              conv_w[l], conv_b[l], conv_ln_g[l], conv_ln_b[l], w_out[l], b_out[l],
                              ln1_g[l], ln1_b[l], w_router[l], router_bias[l], w_gate_up[l], w_down[l],
                              ws_gate_up[l], ws_down[l], ln2_g[l], ln2_b[l])
        return x

    y_prompt = trunk(x_prompt, c_prompt)
    y_sample = trunk(x_sample, c_sample)
    return (y_prompt, y_sample)
```

```python
import functools
import math

import jax
import jax.numpy as jnp
from jax import lax
from jax.experimental import pallas as pl
from jax.experimental.pallas import tpu as pltpu

F32 = jnp.float32
BF16 = jnp.bfloat16

HEAD_DIM = 64
V_DIM = 2 * HEAD_DIM
CONV_TAPS = 31
CONV_HALF = CONV_TAPS // 2
CONV_HALO = 16
N_EXPERTS = 64
N_GROUPS = 8
GROUP_SIZE = N_EXPERTS // N_GROUPS
TOPK_GROUPS = 4
TOP_K = 8
ROUTED_SCALE = 2.5
LN_EPS = 1e-5

V7X_VMEM_BYTES = 64 * 1024 * 1024
VMEM_LIMIT = V7X_VMEM_BYTES - 12 * 1024 * 1024


def _cparams(sem):
    return pltpu.CompilerParams(dimension_semantics=sem, vmem_limit_bytes=VMEM_LIMIT)


def _tile(n, pref, align=8):
    if n <= pref:
        return n
    for t in range(pref - pref % align, 0, -align):
        if n % t == 0:
            return t
    raise ValueError((n, pref, align))


def _ln_rows(v, g, b):
    mu = jnp.mean(v, axis=-1, keepdims=True)
    d = v - mu
    var = jnp.mean(d * d, axis=-1, keepdims=True)
    return d * lax.rsqrt(var + LN_EPS) * g + b


def _silu(v):
    return v * jax.nn.sigmoid(v)


def _mod_kernel(c_ref, w_ref, b_ref, o_ref):
    a = _silu(c_ref[...]).astype(BF16)
    o_ref[...] = jnp.dot(a, w_ref[...].astype(BF16), preferred_element_type=F32) + b_ref[...]


def ada_mod(c, w_ada, b_ada):
    L, D, N = w_ada.shape
    Bt = c.shape[0]
    tn = _tile(N, 1024, 128)
    return pl.pallas_call(
        _mod_kernel,
        grid=(L, N // tn),
        in_specs=[
            pl.BlockSpec((Bt, D), lambda l, j: (0, 0)),
            pl.BlockSpec((None, D, tn), lambda l, j: (l, 0, j)),
            pl.BlockSpec((None, 1, tn), lambda l, j: (l, 0, j)),
        ],
        out_specs=pl.BlockSpec((None, Bt, tn), lambda l, j: (l, 0, j)),
        out_shape=jax.ShapeDtypeStruct((L, Bt, N), F32),
        compiler_params=_cparams(("parallel", "parallel")),
        name="ada_mod",
    )(c, w_ada, b_ada.reshape(L, 1, N))


def _ln_kernel(x_ref, g_ref, b_ref, o_ref):
    o_ref[...] = _ln_rows(x_ref[...], g_ref[...], b_ref[...])


def layer_norm_rows(x, g, b):
    T, D = x.shape
    tm = _tile(T, 512)
    return pl.pallas_call(
        _ln_kernel,
        grid=(T // tm,),
        in_specs=[
            pl.BlockSpec((tm, D), lambda i: (i, 0)),
            pl.BlockSpec((1, D), lambda i: (0, 0)),
            pl.BlockSpec((1, D), lambda i: (0, 0)),
        ],
        out_specs=pl.BlockSpec((tm, D), lambda i: (i, 0)),
        out_shape=jax.ShapeDtypeStruct((T, D), F32),
        compiler_params=_cparams(("parallel",)),
        name="emb_ln",
    )(x, g.reshape(1, D), b.reshape(1, D))


def _inproj_kernel(x_ref, sc_ref, sh_ref, w_ref, b_ref, o_ref, h_scr):
    @pl.when(pl.program_id(1) == 0)
    def _():
        h_scr[...] = (x_ref[...] * (1.0 + sc_ref[...]) + sh_ref[...]).astype(BF16)

    acc = jnp.dot(h_scr[...], w_ref[...], preferred_element_type=F32)
    o_ref[...] = (acc + b_ref[...]).astype(o_ref.dtype)


def in_proj(x, sc, sh, w, b, seq):
    T, D = x.shape
    N = w.shape[1]
    tm = _tile(seq, 512)
    tn = _tile(N, 1024, 128)
    per_b = seq // tm
    return pl.pallas_call(
        _inproj_kernel,
        grid=(T // tm, N // tn),
        in_specs=[
            pl.BlockSpec((tm, D), lambda i, j: (i, 0)),
            pl.BlockSpec((None, 1, D), lambda i, j: (i // per_b, 0, 0)),
            pl.BlockSpec((None, 1, D), lambda i, j: (i // per_b, 0, 0)),
            pl.BlockSpec((D, tn), lambda i, j: (0, j)),
            pl.BlockSpec((1, tn), lambda i, j: (0, j)),
        ],
        out_specs=pl.BlockSpec((tm, tn), lambda i, j: (i, j)),
        out_shape=jax.ShapeDtypeStruct((T, N), BF16),
        scratch_shapes=[pltpu.VMEM((tm, D), BF16)],
        compiler_params=_cparams(("parallel", "arbitrary")),
        name="in_proj",
    )(x, sc, sh, w, b.reshape(1, N))


def _attn_kernel(lam_ref, q_ref, k_ref, v_ref, g_ref, o_ref, *, out_scale):
    h = pl.program_id(1)
    qi = pl.program_id(2)
    tq = q_ref.shape[0]
    S = k_ref.shape[0]
    lam = lam_ref[0, 0]
    slope = jnp.exp2(-(jnp.full((1, 1), h, jnp.int32) + 1).astype(F32) * (8.0 / pl.num_programs(1)))
    qpos = qi * tq + lax.broadcasted_iota(jnp.int32, (tq, S), 0)
    kpos = lax.broadcasted_iota(jnp.int32, (tq, S), 1)
    bias = -slope * jnp.abs(qpos - kpos).astype(F32)

    q = q_ref[...]
    k = k_ref[...]
    scale = HEAD_DIM ** -0.5

    def softmax_map(j):
        s = lax.dot_general(q[:, j * HEAD_DIM:(j + 1) * HEAD_DIM], k[:, j * HEAD_DIM:(j + 1) * HEAD_DIM],
                            (((1,), (1,)), ((), ())), preferred_element_type=F32)
        s = s * scale + bias
        m = jnp.max(s, axis=-1, keepdims=True)
        p = jnp.exp(s - m)
        return p / jnp.sum(p, axis=-1, keepdims=True)

    p = softmax_map(0) - lam * softmax_map(1)
    o = jnp.dot(p.astype(BF16), v_ref[...], preferred_element_type=F32)
    o = o * lax.rsqrt(jnp.mean(o * o, axis=-1, keepdims=True) + LN_EPS) * g_ref[...] * out_scale
    o_ref[...] = o.astype(o_ref.dtype)


def diff_attention(P, lam, subln_g, batch, seq, n_heads, out_scale):
    T = P.shape[0]
    tq = _tile(seq, 128)
    nq = seq // tq
    return pl.pallas_call(
        functools.partial(_attn_kernel, out_scale=out_scale),
        grid=(batch, n_heads, nq),
        in_specs=[
            pl.BlockSpec(memory_space=pltpu.SMEM),
            pl.BlockSpec((tq, V_DIM), lambda b, h, i: (b * nq + i, h)),
            pl.BlockSpec((seq, V_DIM), lambda b, h, i: (b, n_heads + h)),
            pl.BlockSpec((seq, V_DIM), lambda b, h, i: (b, 2 * n_heads + h)),
            pl.BlockSpec((1, V_DIM), lambda b, h, i: (0, 0)),
        ],
        out_specs=pl.BlockSpec((tq, V_DIM), lambda b, h, i: (b * nq + i, h)),
        out_shape=jax.ShapeDtypeStruct((T, n_heads * V_DIM), BF16),
        compiler_params=_cparams(("parallel", "parallel", "arbitrary")),
        name="diff_attn",
    )(lam.reshape(1, 1), P, P, P, subln_g.reshape(1, V_DIM))


CONV_ROW_CHUNK = 32
LANES = 128


def _conv_kernel(ap_ref, ac_ref, an_ref, gp_ref, gc_ref, gn_ref, w_ref, cb_ref, lg_ref, lb_ref,
                 o_ref, u_scr, y_scr):
    i = pl.program_id(1)
    ts, C = o_ref.shape

    def glu(a_ref, g_ref):
        return a_ref[...].astype(F32) * jax.nn.sigmoid(g_ref[...].astype(F32))

    u_scr[0:CONV_HALO, :] = jnp.where(i > 0, glu(ap_ref, gp_ref), 0.0)
    u_scr[CONV_HALO:CONV_HALO + ts, :] = glu(ac_ref, gc_ref)
    u_scr[CONV_HALO + ts:, :] = jnp.where(i < pl.num_programs(1) - 1, glu(an_ref, gn_ref), 0.0)

    first = CONV_HALO - CONV_HALF
    for c0 in range(0, C, LANES):
        wv = w_ref[:, c0:c0 + LANES]
        cb = cb_ref[:, c0:c0 + LANES]
        for r0 in range(0, ts, CONV_ROW_CHUNK):
            acc = jnp.zeros((CONV_ROW_CHUNK, LANES), F32)
            for t in range(CONV_TAPS):
                acc = acc + u_scr[r0 + first + t:r0 + first + t + CONV_ROW_CHUNK, c0:c0 + LANES] * wv[t:t + 1, :]
            y_scr[r0:r0 + CONV_ROW_CHUNK, c0:c0 + LANES] = acc + cb

    y = _ln_rows(y_scr[...], lg_ref[...], lb_ref[...])
    o_ref[...] = _silu(y).astype(o_ref.dtype)


def conformer_conv(P, conv_w, conv_b, ln_g, ln_b, batch, seq, col_a, col_g):
    T = P.shape[0]
    C = conv_w.shape[1]
    ts = _tile(seq, 128)
    ns = seq // ts
    hb = ts // CONV_HALO
    last_halo = T // CONV_HALO - 1

    def cur(col):
        return pl.BlockSpec((ts, C), lambda b, i: (b * ns + i, col))

    def prev(col):
        return pl.BlockSpec((CONV_HALO, C), lambda b, i: (jnp.maximum((b * ns + i) * hb - 1, 0), col))

    def nxt(col):
        return pl.BlockSpec((CONV_HALO, C), lambda b, i: (jnp.minimum((b * ns + i + 1) * hb, last_halo), col))

    vec = pl.BlockSpec((1, C), lambda b, i: (0, 0))
    return pl.pallas_call(
        _conv_kernel,
        grid=(batch, ns),
        in_specs=[prev(col_a), cur(col_a), nxt(col_a), prev(col_g), cur(col_g), nxt(col_g),
                  pl.BlockSpec((CONV_TAPS, C), lambda b, i: (0, 0)), vec, vec, vec],
        out_specs=pl.BlockSpec((ts, C), lambda b, i: (b * ns + i, 0)),
        out_shape=jax.ShapeDtypeStruct((T, C), BF16),
        scratch_shapes=[pltpu.VMEM((ts + 2 * CONV_HALO, C), F32), pltpu.VMEM((ts, C), F32)],
        compiler_params=_cparams(("parallel", "arbitrary")),
        name="conformer_conv",
    )(P, P, P, P, P, P, conv_w, conv_b.reshape(1, C), ln_g.reshape(1, C), ln_b.reshape(1, C))


def _outproj_kernel(at_ref, cv_ref, x_ref, w_ref, b_ref, ga_ref, l1g_ref, l1b_ref, sc_ref, sh_ref, wr_ref,
                    x1_ref, h2_ref, lg_ref, *, alpha):
    aw = at_ref.shape[1]
    mix = jnp.dot(at_ref[...], w_ref[0:aw, :], preferred_element_type=F32)
    mix = mix + jnp.dot(cv_ref[...], w_ref[aw:, :], preferred_element_type=F32) + b_ref[...]
    x1 = _ln_rows(alpha * x_ref[...] + ga_ref[...] * mix, l1g_ref[...], l1b_ref[...])
    x1_ref[...] = x1
    h2 = x1 * (1.0 + sc_ref[...]) + sh_ref[...]
    h2_ref[...] = h2
    lg_ref[...] = lax.dot_general(wr_ref[...], h2, (((1,), (1,)), ((), ())),
                                  precision=lax.Precision.HIGHEST, preferred_element_type=F32)


def out_proj(attn, conv, x, w_out, b_out, g_a, ln_g, ln_b, sc_f, sh_f, w_router_t, seq, alpha):
    T, D = x.shape
    aw = attn.shape[1]
    cw = conv.shape[1]
    E = w_router_t.shape[0]
    tm = _tile(seq, 256)
    per_b = seq // tm
    vec = pl.BlockSpec((1, D), lambda i: (0, 0))
    bvec = pl.BlockSpec((None, 1, D), lambda i: (i // per_b, 0, 0))
    row = pl.BlockSpec((tm, D), lambda i: (i, 0))
    return pl.pallas_call(
        functools.partial(_outproj_kernel, alpha=alpha),
        grid=(T // tm,),
        in_specs=[
            pl.BlockSpec((tm, aw), lambda i: (i, 0)),
            pl.BlockSpec((tm, cw), lambda i: (i, 0)),
            row,
            pl.BlockSpec((aw + cw, D), lambda i: (0, 0)),
            vec, bvec, vec, vec, bvec, bvec,
            pl.BlockSpec((E, D), lambda i: (0, 0)),
        ],
        out_specs=[row, row, pl.BlockSpec((E, tm), lambda i: (0, i))],
        out_shape=[jax.ShapeDtypeStruct((T, D), F32), jax.ShapeDtypeStruct((T, D), F32),
                   jax.ShapeDtypeStruct((E, T), F32)],
        compiler_params=_cparams(("parallel",)),
        name="out_proj",
    )(attn, conv, x, w_out, b_out.reshape(1, D), g_a, ln_g.reshape(1, D), ln_b.reshape(1, D), sc_f, sh_f,
      w_router_t)


def _route_kernel(lg_ref, rb_ref, eid_ref, gate_ref, pos_ref, cnt_ref, carry):
    step = pl.program_id(0)
    tn = lg_ref.shape[1]

    @pl.when(step == 0)
    def _():
        carry[...] = jnp.zeros_like(carry)

    grp = lax.broadcasted_iota(jnp.int32, (N_GROUPS, tn), 0)
    scores, sel, ids = [], [], []
    for i in range(GROUP_SIZE):
        sc = jax.nn.sigmoid(lg_ref[i * N_GROUPS:(i + 1) * N_GROUPS, :])
        scores.append(sc)
        sel.append(sc + rb_ref[i * N_GROUPS:(i + 1) * N_GROUPS, :])
        ids.append(grp * GROUP_SIZE + i)

    top1 = sel[0]
    top2 = jnp.full_like(top1, -jnp.inf)
    for i in range(1, GROUP_SIZE):
        top2 = jnp.maximum(top2, jnp.minimum(top1, sel[i]))
        top1 = jnp.maximum(top1, sel[i])
    gscore = top1 + top2

    rank = jnp.zeros((N_GROUPS, tn), jnp.int32)
    for r in range(1, N_GROUPS):
        other = pltpu.roll(gscore, r, axis=0)
        other_idx = pltpu.roll(grp, r, axis=0)
        beats = (other > gscore) | ((other == gscore) & (other_idx < grp))
        rank = rank + beats.astype(jnp.int32)
    gmask = rank < TOPK_GROUPS

    masked = [jnp.where(gmask, s, -jnp.inf) for s in sel]
    onehots, gates = [], []
    for _ in range(TOP_K):
        m = masked[0]
        for i in range(1, GROUP_SIZE):
            m = jnp.maximum(m, masked[i])
        m = jnp.max(m, axis=0, keepdims=True)
        cand = jnp.where(masked[0] == m, ids[0], N_EXPERTS)
        for i in range(1, GROUP_SIZE):
            cand = jnp.minimum(cand, jnp.where(masked[i] == m, ids[i], N_EXPERTS))
        chosen = jnp.min(cand, axis=0, keepdims=True)
        oh = [ids[i] == chosen for i in range(GROUP_SIZE)]
        g = jnp.where(oh[0], scores[0], 0.0)
        for i in range(1, GROUP_SIZE):
            g = g + jnp.where(oh[i], scores[i], 0.0)
        gates.append(jnp.sum(g, axis=0, keepdims=True))
        masked = [jnp.where(oh[i], -jnp.inf, masked[i]) for i in range(GROUP_SIZE)]
        onehots.append(oh)
        eid_ref[pl.ds(len(onehots) - 1, 1), :] = chosen

    denom = gates[0]
    for kk in range(1, TOP_K):
        denom = denom + gates[kk]
    for kk in range(TOP_K):
        gate_ref[pl.ds(kk, 1), :] = gates[kk] / denom * ROUTED_SCALE

    chosen_any = []
    for i in range(GROUP_SIZE):
        a = onehots[0][i]
        for kk in range(1, TOP_K):
            a = a | onehots[kk][i]
        chosen_any.append(a.astype(F32))
    mask = jnp.concatenate(chosen_any, axis=0).astype(BF16)
    tri = (lax.broadcasted_iota(jnp.int32, (tn, tn), 0) <= lax.broadcasted_iota(jnp.int32, (tn, tn), 1))
    csum = jnp.dot(mask, tri.astype(BF16), preferred_element_type=F32) + carry[...]
    for kk in range(TOP_K):
        p = jnp.zeros((N_GROUPS, tn), F32)
        for i in range(GROUP_SIZE):
            p = p + jnp.where(onehots[kk][i], csum[i * N_GROUPS:(i + 1) * N_GROUPS, :], 0.0)
        pos_ref[pl.ds(kk, 1), :] = (jnp.sum(p, axis=0, keepdims=True) - 1.0).astype(jnp.int32)
    total = csum[:, tn - 1:tn]
    carry[...] = total
    cnt_ref[...] = jnp.broadcast_to(total, cnt_ref.shape)


def route(logits_t, bias_rows):
    E, T = logits_t.shape
    tn = _tile(T, 512)
    slot = pl.BlockSpec((TOP_K, tn), lambda i: (0, i))
    return pl.pallas_call(
        _route_kernel,
        grid=(T // tn,),
        in_specs=[pl.BlockSpec((E, tn), lambda i: (0, i)), pl.BlockSpec((E, 1), lambda i: (0, 0))],
        out_specs=[slot, slot, slot, pl.BlockSpec((E, LANES), lambda i: (0, 0))],
        out_shape=[jax.ShapeDtypeStruct((TOP_K, T), jnp.int32), jax.ShapeDtypeStruct((TOP_K, T), F32),
                   jax.ShapeDtypeStruct((TOP_K, T), jnp.int32), jax.ShapeDtypeStruct((E, LANES), F32)],
        scratch_shapes=[pltpu.VMEM((E, 1), F32)],
        compiler_params=_cparams(("arbitrary",)),
        name="route",
    )(logits_t, bias_rows)


EXPERT_ROWS = 256


def _experts_kernel(blk_e_ref, blk_n_ref, tok_ref, gate_ref, h_hbm, wgu_ref, wd_ref, o_ref, buf, sem):
    b = pl.program_id(0)
    R = buf.shape[0]
    F = wd_ref.shape[0]

    def row_copy(r):
        return pltpu.make_async_copy(h_hbm.at[pl.ds(tok_ref[0, 0, r], 1), :], buf.at[pl.ds(r, 1), :], sem)

    @pl.when(blk_n_ref[b] > 0)
    def _():
        def start(r, c):
            row_copy(r).start()
            return c

        lax.fori_loop(0, R, start, 0)

        def wait(r, c):
            row_copy(r).wait()
            return c

        lax.fori_loop(0, R, wait, 0)
        x = buf[...].astype(BF16)
        gu = jnp.dot(x, wgu_ref[...], preferred_element_type=F32)
        act = (_silu(gu[:, :F]) * gu[:, F:]).astype(BF16)
        y = jnp.dot(act, wd_ref[...], preferred_element_type=F32)
        o_ref[...] = y * gate_ref[...]

    @pl.when(blk_n_ref[b] == 0)
    def _():
        o_ref[...] = jnp.zeros_like(o_ref)


def routed_experts(h2, row_tok, row_gate, blk_e, blk_n, w_gu, w_d):
    T, D = h2.shape
    n_rows = row_tok.shape[0]
    R = EXPERT_ROWS
    n_blk = n_rows // R
    F = w_d.shape[1]
    grid_spec = pltpu.PrefetchScalarGridSpec(
        num_scalar_prefetch=2,
        grid=(n_blk,),
        in_specs=[
            pl.BlockSpec((1, 1, R), lambda b, be, bn: (b, 0, 0), memory_space=pltpu.SMEM),
            pl.BlockSpec((R, 1), lambda b, be, bn: (b, 0)),
            pl.BlockSpec(memory_space=pl.ANY),
            pl.BlockSpec((None, D, 2 * F), lambda b, be, bn: (be[b], 0, 0)),
            pl.BlockSpec((None, F, D), lambda b, be, bn: (be[b], 0, 0)),
        ],
        out_specs=pl.BlockSpec((R, D), lambda b, be, bn: (b, 0)),
        scratch_shapes=[pltpu.VMEM((R, D), F32), pltpu.SemaphoreType.DMA],
    )
    return pl.pallas_call(
        _experts_kernel,
        grid_spec=grid_spec,
        out_shape=jax.ShapeDtypeStruct((n_rows, D), F32),
        compiler_params=_cparams(("arbitrary",)),
        name="routed_experts",
    )(blk_e, blk_n, row_tok.reshape(n_blk, 1, R), row_gate.reshape(n_rows, 1), h2, w_gu, w_d)


COMBINE_ROWS = 128


def _combine_kernel(dest_ref, y_hbm, h_ref, x_ref, wgu_ref, wd_ref, gf_ref, lg_ref, lb_ref, o_ref, buf, sem,
                    *, alpha):
    tm = h_ref.shape[0]
    F = wd_ref.shape[0]

    def row_copy(n):
        kk = n // tm
        r = n - kk * tm
        return pltpu.make_async_copy(y_hbm.at[pl.ds(dest_ref[0, 0, n], 1), :], buf.at[kk, pl.ds(r, 1), :], sem)

    def start(n, c):
        row_copy(n).start()
        return c

    lax.fori_loop(0, TOP_K * tm, start, 0)

    gu = jnp.dot(h_ref[...].astype(BF16), wgu_ref[...], preferred_element_type=F32)
    act = (_silu(gu[:, :F]) * gu[:, F:]).astype(BF16)
    ff = jnp.dot(act, wd_ref[...], preferred_element_type=F32)

    def wait(n, c):
        row_copy(n).wait()
        return c

    lax.fori_loop(0, TOP_K * tm, wait, 0)
    for kk in range(TOP_K):
        ff = ff + buf[kk]
    o_ref[...] = _ln_rows(alpha * x_ref[...] + gf_ref[...] * ff, lg_ref[...], lb_ref[...])


def combine(dest, y_rows, h2, x1, ws_gu, ws_d, g_f, ln_g, ln_b, seq, alpha):
    T, D = x1.shape
    F = ws_d.shape[0]
    tm = _tile(seq, COMBINE_ROWS)
    per_b = seq // tm
    n_t = T // tm
    row = pl.BlockSpec((tm, D), lambda i: (i, 0))
    vec = pl.BlockSpec((1, D), lambda i: (0, 0))
    return pl.pallas_call(
        functools.partial(_combine_kernel, alpha=alpha),
        grid=(n_t,),
        in_specs=[
            pl.BlockSpec((1, 1, TOP_K * tm), lambda i: (i, 0, 0), memory_space=pltpu.SMEM),
            pl.BlockSpec(memory_space=pl.ANY),
            row, row,
            pl.BlockSpec((D, 2 * F), lambda i: (0, 0)),
            pl.BlockSpec((F, D), lambda i: (0, 0)),
            pl.BlockSpec((None, 1, D), lambda i: (i // per_b, 0, 0)),
            vec, vec,
        ],
        out_specs=row,
        out_shape=jax.ShapeDtypeStruct((T, D), F32),
        scratch_shapes=[pltpu.VMEM((TOP_K, tm, D), F32), pltpu.SemaphoreType.DMA],
        compiler_params=_cparams(("arbitrary",)),
        name="combine",
    )(dest, y_rows, h2, x1, ws_gu, ws_d, g_f, ln_g.reshape(1, D), ln_b.reshape(1, D))


def _dispatch_tables(eid, gate, pos, counts_rows, T):
    R = EXPERT_ROWS
    n_rows = T * TOP_K + N_EXPERTS * R
    n_blk = n_rows // R
    rows = jnp.arange(N_EXPERTS)
    counts = jnp.zeros((N_EXPERTS,), jnp.int32).at[(rows % N_GROUPS) * GROUP_SIZE + rows // N_GROUPS].set(
        counts_rows.astype(jnp.int32))
    padded = (counts + R - 1) // R * R
    pad_ends = jnp.cumsum(padded)
    pad_starts = pad_ends - padded
    dest = pad_starts[eid] + pos
    tok = jnp.broadcast_to(jnp.arange(T, dtype=jnp.int32)[None, :], dest.shape)
    row_tok = jnp.zeros((n_rows,), jnp.int32).at[dest.reshape(-1)].set(tok.reshape(-1))
    row_gate = jnp.zeros((n_rows,), F32).at[dest.reshape(-1)].set(gate.reshape(-1))
    blk_start = jnp.arange(n_blk, dtype=jnp.int32) * R
    blk_e = jnp.minimum(jnp.searchsorted(pad_ends, blk_start, side="right"), N_EXPERTS - 1).astype(jnp.int32)
    blk_n = jnp.clip(pad_starts[blk_e] + counts[blk_e] - blk_start, 0, R).astype(jnp.int32)
    return dest, row_tok, row_gate, blk_e, blk_n


def _combine_index(dest, tm):
    K, T = dest.shape
    return dest.reshape(K, T // tm, tm).transpose(1, 0, 2).reshape(T // tm, 1, K * tm)


def _trunk(x, mods, lams, p, depth):
    B, S, D = x.shape
    T = B * S
    aw = D // 2
    n_heads = aw // V_DIM
    alpha = (2 * depth) ** 0.25
    x = layer_norm_rows(x.reshape(T, D), p["emb_ln_g"], p["emb_ln_b"])
    for l in range(depth):
        mod = mods[l]
        sh_a, sc_a, g_a, sh_f, sc_f, g_f = (mod[:, i] for i in range(6))
        P = in_proj(x, sc_a, sh_a, p["w_in"][l], p["b_in"][l], S)
        lam_init = 0.8 - 0.6 * math.exp(-0.3 * l)
        attn = diff_attention(P, lams[l], p["attn_subln_g"][l], B, S, n_heads, 1.0 - lam_init)
        conv = conformer_conv(P, p["conv_w"][l], p["conv_b"][l], p["conv_ln_g"][l], p["conv_ln_b"][l],
                              B, S, 3 * aw // (D - aw), 3 * aw // (D - aw) + 1)
        x1, h2, logits_t = out_proj(attn, conv, x, p["w_out"][l], p["b_out"][l], g_a, p["ln1_g"][l], p["ln1_b"][l],
                                    sc_f, sh_f, p["w_router_t"][l], S, alpha)
        eid, gate, pos, counts_rows = route(logits_t, p["router_bias_rows"][l])
        dest, row_tok, row_gate, blk_e, blk_n = _dispatch_tables(eid, gate, pos, counts_rows[:, 0], T)
        y_rows = routed_experts(h2, row_tok, row_gate, blk_e, blk_n, p["w_gate_up"][l], p["w_down"][l])
        tm = _tile(S, COMBINE_ROWS)
        x = combine(_combine_index(dest, tm), y_rows, h2, x1, p["ws_gate_up"][l], p["ws_down"][l], g_f,
                    p["ln2_g"][l], p["ln2_b"][l], S, alpha)
    return x.reshape(B, S, D)


def kernel(x_prompt, x_sample, c_prompt, c_sample, emb_ln_g, emb_ln_b, w_ada, b_ada, w_in, b_in, lambda_q1, lambda_k1, lambda_q2, lambda_k2, attn_subln_g, conv_w, conv_b, conv_ln_g, conv_ln_b, w_out, b_out, ln1_g, ln1_b, w_router, router_bias, w_gate_up, w_down, ws_gate_up, ws_down, ln2_g, ln2_b):
    depth, D = w_in.shape[0], w_in.shape[1]
    Bp, Bs = x_prompt.shape[0], x_sample.shape[0]

    c_all = jnp.concatenate([c_prompt, c_sample], axis=0)
    pad = (-c_all.shape[0]) % 8
    c_all = jnp.pad(c_all, ((0, pad), (0, 0)))
    mod_all = ada_mod(c_all, w_ada, b_ada)
    mod_all = mod_all.reshape(depth, c_all.shape[0], 6, 1, D)
    lams = (jnp.exp(jnp.sum(lambda_q1 * lambda_k1, axis=-1)) - jnp.exp(jnp.sum(lambda_q2 * lambda_k2, axis=-1))
            + jnp.asarray([0.8 - 0.6 * math.exp(-0.3 * l) for l in range(depth)], F32))

    rows = jnp.arange(N_EXPERTS)
    perm = (rows % N_GROUPS) * GROUP_SIZE + rows // N_GROUPS
    params = dict(
        emb_ln_g=emb_ln_g, emb_ln_b=emb_ln_b,
        w_in=w_in.astype(BF16), b_in=b_in, attn_subln_g=attn_subln_g,
        conv_w=conv_w, conv_b=conv_b, conv_ln_g=conv_ln_g, conv_ln_b=conv_ln_b,
        w_out=w_out.astype(BF16), b_out=b_out, ln1_g=ln1_g, ln1_b=ln1_b,
        w_router_t=jnp.swapaxes(w_router, 1, 2)[:, perm, :],
        router_bias_rows=router_bias[:, perm].reshape(depth, N_EXPERTS, 1),
        w_gate_up=w_gate_up.astype(BF16), w_down=w_down.astype(BF16),
        ws_gate_up=ws_gate_up.astype(BF16), ws_down=ws_down.astype(BF16),
        ln2_g=ln2_g, ln2_b=ln2_b,
    )
    y_prompt = _trunk(x_prompt, mod_all[:, :Bp], lams, params, depth)
    y_sample = _trunk(x_sample, mod_all[:, Bp:Bp + Bs], lams, params, depth)
    return (y_prompt, y_sample)
```

```python
import functools
import math

import jax
import jax.numpy as jnp
from jax import lax
from jax.experimental import pallas as pl
from jax.experimental.pallas import tpu as pltpu

F32 = jnp.float32
BF16 = jnp.bfloat16

HEAD_DIM = 64
V_DIM = 2 * HEAD_DIM
CONV_TAPS = 31
CONV_HALF = CONV_TAPS // 2
CONV_HALO = 16
N_EXPERTS = 64
N_GROUPS = 8
GROUP_SIZE = N_EXPERTS // N_GROUPS
TOPK_GROUPS = 4
TOP_K = 8
ROUTED_SCALE = 2.5
LN_EPS = 1e-5

V7X_VMEM_BYTES = 64 * 1024 * 1024
VMEM_LIMIT = V7X_VMEM_BYTES - 12 * 1024 * 1024


def _cparams(sem):
    return pltpu.CompilerParams(dimension_semantics=sem, vmem_limit_bytes=VMEM_LIMIT)


def _tile(n, pref, align=8):
    if n <= pref:
        return n
    for t in range(pref - pref % align, 0, -align):
        if n % t == 0:
            return t
    raise ValueError((n, pref, align))


def _ln_rows(v, g, b):
    mu = jnp.mean(v, axis=-1, keepdims=True)
    d = v - mu
    var = jnp.mean(d * d, axis=-1, keepdims=True)
    return d * lax.rsqrt(var + LN_EPS) * g + b


def _silu(v):
    return v * jax.nn.sigmoid(v)


def _mod_kernel(c_ref, w_ref, b_ref, o_ref):
    a = _silu(c_ref[...]).astype(BF16)
    o_ref[...] = jnp.dot(a, w_ref[...].astype(BF16), preferred_element_type=F32) + b_ref[...]


def ada_mod(c, w_ada, b_ada):
    L, D, N = w_ada.shape
    Bt = c.shape[0]
    tn = _tile(N, 1024, 128)
    return pl.pallas_call(
        _mod_kernel,
        grid=(L, N // tn),
        in_specs=[
            pl.BlockSpec((Bt, D), lambda l, j: (0, 0)),
            pl.BlockSpec((None, D, tn), lambda l, j: (l, 0, j)),
            pl.BlockSpec((None, 1, tn), lambda l, j: (l, 0, j)),
        ],
        out_specs=pl.BlockSpec((None, Bt, tn), lambda l, j: (l, 0, j)),
        out_shape=jax.ShapeDtypeStruct((L, Bt, N), F32),
        compiler_params=_cparams(("parallel", "parallel")),
        name="ada_mod",
    )(c, w_ada, b_ada.reshape(L, 1, N))


def _ln_kernel(x_ref, g_ref, b_ref, o_ref):
    o_ref[...] = _ln_rows(x_ref[...], g_ref[...], b_ref[...])


def layer_norm_rows(x, g, b):
    T, D = x.shape
    tm = _tile(T, 512)
    return pl.pallas_call(
        _ln_kernel,
        grid=(T // tm,),
        in_specs=[
            pl.BlockSpec((tm, D), lambda i: (i, 0)),
            pl.BlockSpec((1, D), lambda i: (0, 0)),
            pl.BlockSpec((1, D), lambda i: (0, 0)),
        ],
        out_specs=pl.BlockSpec((tm, D), lambda i: (i, 0)),
        out_shape=jax.ShapeDtypeStruct((T, D), F32),
        compiler_params=_cparams(("parallel",)),
        name="emb_ln",
    )(x, g.reshape(1, D), b.reshape(1, D))


def _inproj_kernel(x_ref, sc_ref, sh_ref, w_ref, b_ref, o_ref, h_scr):
    @pl.when(pl.program_id(1) == 0)
    def _():
        h_scr[...] = (x_ref[...] * (1.0 + sc_ref[...]) + sh_ref[...]).astype(BF16)

    acc = jnp.dot(h_scr[...], w_ref[...], preferred_element_type=F32)
    o_ref[...] = (acc + b_ref[...]).astype(o_ref.dtype)


def in_proj(x, sc, sh, w, b, seq):
    T, D = x.shape
    N = w.shape[1]
    tm = _tile(seq, 512)
    tn = _tile(N, 1024, 128)
    per_b = seq // tm
    return pl.pallas_call(
        _inproj_kernel,
        grid=(T // tm, N // tn),
        in_specs=[
            pl.BlockSpec((tm, D), lambda i, j: (i, 0)),
            pl.BlockSpec((None, 1, D), lambda i, j: (i // per_b, 0, 0)),
            pl.BlockSpec((None, 1, D), lambda i, j: (i // per_b, 0, 0)),
            pl.BlockSpec((D, tn), lambda i, j: (0, j)),
            pl.BlockSpec((1, tn), lambda i, j: (0, j)),
        ],
        out_specs=pl.BlockSpec((tm, tn), lambda i, j: (i, j)),
        out_shape=jax.ShapeDtypeStruct((T, N), BF16),
        scratch_shapes=[pltpu.VMEM((tm, D), BF16)],
        compiler_params=_cparams(("parallel", "arbitrary")),
        name="in_proj",
    )(x, sc, sh, w, b.reshape(1, N))


ATTN_TILE = 256
POS_SPLIT = 256
EXP_CHUNK = 512


def _attn_kernel(lam_ref, q_ref, k_ref, v_ref, g_ref, o_ref, qe_scr, ke_scr, ve_scr, s_scr, p_scr,
                 *, n_heads, out_scale):
    h = pl.program_id(1)
    qi = pl.program_id(2)
    tq = q_ref.shape[0]
    S = k_ref.shape[0]
    nk = S // tq
    lam = lam_ref[0, 0]
    slope = jnp.exp2(-(jnp.full((1, 1), h, jnp.int32) + 1).astype(F32) * (8.0 / n_heads))
    nt_dims = (((1,), (1,)), ((), ()))

    def features(pos, a, b):
        lane = lax.broadcasted_iota(jnp.int32, pos.shape, 1)
        hi = (pos // POS_SPLIT).astype(F32) * (slope * POS_SPLIT)
        lo = (pos % POS_SPLIT).astype(F32) * slope
        f = jnp.where(lane == HEAD_DIM + a[0], hi, 0.0)
        f = jnp.where(lane == HEAD_DIM + a[1], lo, f)
        return jnp.where((lane == HEAD_DIM + b[0]) | (lane == HEAD_DIM + b[1]), 1.0, f)

    @pl.when(qi == 0)
    def _():
        kpos = lax.broadcasted_iota(jnp.int32, (S, V_DIM), 0)
        lane = lax.broadcasted_iota(jnp.int32, (S, V_DIM), 1)
        kfeat = features(kpos, (2, 3), (0, 1))
        kf = k_ref[...].astype(F32)
        ke_scr[0] = jnp.where(lane < HEAD_DIM, kf, kfeat).astype(BF16)
        ke_scr[1] = jnp.where(lane < HEAD_DIM, pltpu.roll(kf, HEAD_DIM, axis=1), kfeat).astype(BF16)
        ve_scr[:, 0:V_DIM] = v_ref[...]
        ve_scr[:, V_DIM:] = jnp.where(lane == 0, 1.0, 0.0).astype(BF16)

    qpos = qi * tq + lax.broadcasted_iota(jnp.int32, (tq, V_DIM), 0)
    lane = lax.broadcasted_iota(jnp.int32, (tq, V_DIM), 1)
    qfeat = features(qpos, (0, 1), (2, 3))
    qfeat = jnp.where(lane < HEAD_DIM + 2, -qfeat, qfeat)
    qf = q_ref[...].astype(F32) * (HEAD_DIM ** -0.5)
    for j in range(2):
        qj = qf if j == 0 else pltpu.roll(qf, HEAD_DIM, axis=1)
        qe_scr[0, j] = jnp.where(lane < HEAD_DIM, qj, qfeat).astype(BF16)
        qe_scr[1, j] = jnp.where(lane < HEAD_DIM, qj, -qfeat).astype(BF16)
        qe_scr[2, j] = jnp.where(lane < HEAD_DIM, qj, 0.0).astype(BF16)

    rel = (lax.broadcasted_iota(jnp.int32, (tq, tq), 0) - lax.broadcasted_iota(jnp.int32, (tq, tq), 1))
    diag_bias = -slope * jnp.abs(rel).astype(F32)

    def scores(j):
        for r in range(nk):
            c = qi + r
            wrapped = c >= nk
            c = jnp.where(wrapped, c - nk, c)
            off = pl.multiple_of(c * tq, tq)
            variant = 2 if r == 0 else jnp.where(wrapped, 0, 1)
            s = lax.dot_general(qe_scr[variant, j], ke_scr[j, pl.ds(off, tq), :], nt_dims,
                                preferred_element_type=F32)
            s_scr[j, :, pl.ds(off, tq)] = s + diag_bias if r == 0 else s

    def row_max(j):
        return jnp.max(s_scr[j], axis=1, keepdims=True)

    def exps(j, m):
        w = min(EXP_CHUNK, S)
        for c in range(S // w):
            p_scr[j, :, c * w:(c + 1) * w] = jnp.exp(s_scr[j, :, c * w:(c + 1) * w] - m).astype(BF16)

    def weighted_values(j):
        acc = jnp.dot(p_scr[j], ve_scr[...], preferred_element_type=F32)
        return acc[:, 0:V_DIM] / acc[:, V_DIM:V_DIM + 1]

    scores(0)
    scores(1)
    m0 = row_max(0)
    exps(0, m0)
    m1 = row_max(1)
    o0 = weighted_values(0)
    exps(1, m1)
    o1 = weighted_values(1)

    o = o0 - lam * o1
    o = o * lax.rsqrt(jnp.mean(o * o, axis=-1, keepdims=True) + LN_EPS) * g_ref[...] * out_scale
    o_ref[...] = o.astype(o_ref.dtype)


def diff_attention(P, lam, subln_g, batch, seq, n_heads, out_scale):
    T = P.shape[0]
    tq = _tile(seq, ATTN_TILE)
    nq = seq // tq
    return pl.pallas_call(
        functools.partial(_attn_kernel, n_heads=n_heads, out_scale=out_scale),
        grid=(batch, n_heads, nq),
        in_specs=[
            pl.BlockSpec(memory_space=pltpu.SMEM),
            pl.BlockSpec((tq, V_DIM), lambda b, h, i: (b * nq + i, h)),
            pl.BlockSpec((seq, V_DIM), lambda b, h, i: (b, n_heads + h)),
            pl.BlockSpec((seq, V_DIM), lambda b, h, i: (b, 2 * n_heads + h)),
            pl.BlockSpec((1, V_DIM), lambda b, h, i: (0, 0)),
        ],
        out_specs=pl.BlockSpec((tq, V_DIM), lambda b, h, i: (b * nq + i, h)),
        out_shape=jax.ShapeDtypeStruct((T, n_heads * V_DIM), BF16),
        scratch_shapes=[
            pltpu.VMEM((3, 2, tq, V_DIM), BF16),
            pltpu.VMEM((2, seq, V_DIM), BF16),
            pltpu.VMEM((seq, 2 * V_DIM), BF16),
            pltpu.VMEM((2, tq, seq), F32),
            pltpu.VMEM((2, tq, seq), BF16),
        ],
        compiler_params=_cparams(("parallel", "parallel", "arbitrary")),
        name="diff_attn",
    )(lam.reshape(1, 1), P, P, P, subln_g.reshape(1, V_DIM))


CONV_ROW_CHUNK = 32
LANES = 128


def _conv_kernel(ap_ref, ac_ref, an_ref, gp_ref, gc_ref, gn_ref, w_ref, cb_ref, lg_ref, lb_ref,
                 o_ref, u_scr, y_scr):
    i = pl.program_id(1)
    ts, C = o_ref.shape

    def glu(a_ref, g_ref):
        return a_ref[...].astype(F32) * jax.nn.sigmoid(g_ref[...].astype(F32))

    u_scr[0:CONV_HALO, :] = jnp.where(i > 0, glu(ap_ref, gp_ref), 0.0)
    u_scr[CONV_HALO:CONV_HALO + ts, :] = glu(ac_ref, gc_ref)
    u_scr[CONV_HALO + ts:, :] = jnp.where(i < pl.num_programs(1) - 1, glu(an_ref, gn_ref), 0.0)

    first = CONV_HALO - CONV_HALF
    for c0 in range(0, C, LANES):
        wv = w_ref[:, c0:c0 + LANES]
        cb = cb_ref[:, c0:c0 + LANES]
        for r0 in range(0, ts, CONV_ROW_CHUNK):
            acc = jnp.zeros((CONV_ROW_CHUNK, LANES), F32)
            for t in range(CONV_TAPS):
                acc = acc + u_scr[r0 + first + t:r0 + first + t + CONV_ROW_CHUNK, c0:c0 + LANES] * wv[t:t + 1, :]
            y_scr[r0:r0 + CONV_ROW_CHUNK, c0:c0 + LANES] = acc + cb

    y = _ln_rows(y_scr[...], lg_ref[...], lb_ref[...])
    o_ref[...] = _silu(y).astype(o_ref.dtype)


def conformer_conv(P, conv_w, conv_b, ln_g, ln_b, batch, seq, col_a, col_g):
    T = P.shape[0]
    C = conv_w.shape[1]
    ts = _tile(seq, 128)
    ns = seq // ts
    hb = ts // CONV_HALO
    last_halo = T // CONV_HALO - 1

    def cur(col):
        return pl.BlockSpec((ts, C), lambda b, i: (b * ns + i, col))

    def prev(col):
        return pl.BlockSpec((CONV_HALO, C), lambda b, i: (jnp.maximum((b * ns + i) * hb - 1, 0), col))

    def nxt(col):
        return pl.BlockSpec((CONV_HALO, C), lambda b, i: (jnp.minimum((b * ns + i + 1) * hb, last_halo), col))

    vec = pl.BlockSpec((1, C), lambda b, i: (0, 0))
    return pl.pallas_call(
        _conv_kernel,
        grid=(batch, ns),
        in_specs=[prev(col_a), cur(col_a), nxt(col_a), prev(col_g), cur(col_g), nxt(col_g),
                  pl.BlockSpec((CONV_TAPS, C), lambda b, i: (0, 0)), vec, vec, vec],
        out_specs=pl.BlockSpec((ts, C), lambda b, i: (b * ns + i, 0)),
        out_shape=jax.ShapeDtypeStruct((T, C), BF16),
        scratch_shapes=[pltpu.VMEM((ts + 2 * CONV_HALO, C), F32), pltpu.VMEM((ts, C), F32)],
        compiler_params=_cparams(("parallel", "arbitrary")),
        name="conformer_conv",
    )(P, P, P, P, P, P, conv_w, conv_b.reshape(1, C), ln_g.reshape(1, C), ln_b.reshape(1, C))


def _outproj_kernel(at_ref, cv_ref, x_ref, w_ref, b_ref, ga_ref, l1g_ref, l1b_ref, sc_ref, sh_ref, wr_ref,
                    x1_ref, h2_ref, lg_ref, *, alpha):
    aw = at_ref.shape[1]
    mix = jnp.dot(at_ref[...], w_ref[0:aw, :], preferred_element_type=F32)
    mix = mix + jnp.dot(cv_ref[...], w_ref[aw:, :], preferred_element_type=F32) + b_ref[...]
    x1 = _ln_rows(alpha * x_ref[...] + ga_ref[...] * mix, l1g_ref[...], l1b_ref[...])
    x1_ref[...] = x1
    h2 = x1 * (1.0 + sc_ref[...]) + sh_ref[...]
    h2_ref[...] = h2
    lg_ref[...] = lax.dot_general(wr_ref[...], h2, (((1,), (1,)), ((), ())),
                                  precision=lax.Precision.HIGHEST, preferred_element_type=F32)


def out_proj(attn, conv, x, w_out, b_out, g_a, ln_g, ln_b, sc_f, sh_f, w_router_t, seq, alpha):
    T, D = x.shape
    aw = attn.shape[1]
    cw = conv.shape[1]
    E = w_router_t.shape[0]
    tm = _tile(seq, 256)
    per_b = seq // tm
    vec = pl.BlockSpec((1, D), lambda i: (0, 0))
    bvec = pl.BlockSpec((None, 1, D), lambda i: (i // per_b, 0, 0))
    row = pl.BlockSpec((tm, D), lambda i: (i, 0))
    return pl.pallas_call(
        functools.partial(_outproj_kernel, alpha=alpha),
        grid=(T // tm,),
        in_specs=[
            pl.BlockSpec((tm, aw), lambda i: (i, 0)),
            pl.BlockSpec((tm, cw), lambda i: (i, 0)),
            row,
            pl.BlockSpec((aw + cw, D), lambda i: (0, 0)),
            vec, bvec, vec, vec, bvec, bvec,
            pl.BlockSpec((E, D), lambda i: (0, 0)),
        ],
        out_specs=[row, row, pl.BlockSpec((E, tm), lambda i: (0, i))],
        out_shape=[jax.ShapeDtypeStruct((T, D), F32), jax.ShapeDtypeStruct((T, D), F32),
                   jax.ShapeDtypeStruct((E, T), F32)],
        compiler_params=_cparams(("parallel",)),
        name="out_proj",
    )(attn, conv, x, w_out, b_out.reshape(1, D), g_a, ln_g.reshape(1, D), ln_b.reshape(1, D), sc_f, sh_f,
      w_router_t)


def _route_kernel(lg_ref, rb_ref, eid_ref, gate_ref, pos_ref, cnt_ref, carry):
    step = pl.program_id(0)
    tn = lg_ref.shape[1]

    @pl.when(step == 0)
    def _():
        carry[...] = jnp.zeros_like(carry)

    grp = lax.broadcasted_iota(jnp.int32, (N_GROUPS, tn), 0)
    scores, sel, ids = [], [], []
    for i in range(GROUP_SIZE):
        sc = jax.nn.sigmoid(lg_ref[i * N_GROUPS:(i + 1) * N_GROUPS, :])
        scores.append(sc)
        sel.append(sc + rb_ref[i * N_GROUPS:(i + 1) * N_GROUPS, :])
        ids.append(grp * GROUP_SIZE + i)

    top1 = sel[0]
    top2 = jnp.full_like(top1, -jnp.inf)
    for i in range(1, GROUP_SIZE):
        top2 = jnp.maximum(top2, jnp.minimum(top1, sel[i]))
        top1 = jnp.maximum(top1, sel[i])
    gscore = top1 + top2

    rank = jnp.zeros((N_GROUPS, tn), jnp.int32)
    for r in range(1, N_GROUPS):
        other = pltpu.roll(gscore, r, axis=0)
        other_idx = pltpu.roll(grp, r, axis=0)
        beats = (other > gscore) | ((other == gscore) & (other_idx < grp))
        rank = rank + beats.astype(jnp.int32)
    gmask = rank < TOPK_GROUPS

    masked = [jnp.where(gmask, s, -jnp.inf) for s in sel]
    onehots, gates = [], []
    for _ in range(TOP_K):
        m = masked[0]
        for i in range(1, GROUP_SIZE):
            m = jnp.maximum(m, masked[i])
        m = jnp.max(m, axis=0, keepdims=True)
        cand = jnp.where(masked[0] == m, ids[0], N_EXPERTS)
        for i in range(1, GROUP_SIZE):
            cand = jnp.minimum(cand, jnp.where(masked[i] == m, ids[i], N_EXPERTS))
        chosen = jnp.min(cand, axis=0, keepdims=True)
        oh = [ids[i] == chosen for i in range(GROUP_SIZE)]
        g = jnp.where(oh[0], scores[0], 0.0)
        for i in range(1, GROUP_SIZE):
            g = g + jnp.where(oh[i], scores[i], 0.0)
        gates.append(jnp.sum(g, axis=0, keepdims=True))
        masked = [jnp.where(oh[i], -jnp.inf, masked[i]) for i in range(GROUP_SIZE)]
        onehots.append(oh)
        eid_ref[pl.ds(len(onehots) - 1, 1), :] = chosen

    denom = gates[0]
    for kk in range(1, TOP_K):
        denom = denom + gates[kk]
    for kk in range(TOP_K):
        gate_ref[pl.ds(kk, 1), :] = gates[kk] / denom * ROUTED_SCALE

    chosen_any = []
    for i in range(GROUP_SIZE):
        a = onehots[0][i]
        for kk in range(1, TOP_K):
            a = a | onehots[kk][i]
        chosen_any.append(a.astype(F32))
    mask = jnp.concatenate(chosen_any, axis=0).astype(BF16)
    tri = (lax.broadcasted_iota(jnp.int32, (tn, tn), 0) <= lax.broadcasted_iota(jnp.int32, (tn, tn), 1))
    csum = jnp.dot(mask, tri.astype(BF16), preferred_element_type=F32) + carry[...]
    for kk in range(TOP_K):
        p = jnp.zeros((N_GROUPS, tn), F32)
        for i in range(GROUP_SIZE):
            p = p + jnp.where(onehots[kk][i], csum[i * N_GROUPS:(i + 1) * N_GROUPS, :], 0.0)
        pos_ref[pl.ds(kk, 1), :] = (jnp.sum(p, axis=0, keepdims=True) - 1.0).astype(jnp.int32)
    total = csum[:, tn - 1:tn]
    carry[...] = total
    cnt_ref[...] = jnp.broadcast_to(total, cnt_ref.shape)


def route(logits_t, bias_rows):
    E, T = logits_t.shape
    tn = _tile(T, 512)
    slot = pl.BlockSpec((TOP_K, tn), lambda i: (0, i))
    return pl.pallas_call(
        _route_kernel,
        grid=(T // tn,),
        in_specs=[pl.BlockSpec((E, tn), lambda i: (0, i)), pl.BlockSpec((E, 1), lambda i: (0, 0))],
        out_specs=[slot, slot, slot, pl.BlockSpec((E, LANES), lambda i: (0, 0))],
        out_shape=[jax.ShapeDtypeStruct((TOP_K, T), jnp.int32), jax.ShapeDtypeStruct((TOP_K, T), F32),
                   jax.ShapeDtypeStruct((TOP_K, T), jnp.int32), jax.ShapeDtypeStruct((E, LANES), F32)],
        scratch_shapes=[pltpu.VMEM((E, 1), F32)],
        compiler_params=_cparams(("arbitrary",)),
        name="route",
    )(logits_t, bias_rows)


EXPERT_ROWS = 512
DISPATCH_ROWS = 256
DMA_UNROLL = 8


def _dispatch_kernel(tail_ref, dest_ref, h_ref, xs_hbm, zero_buf, sem):
    tm = h_ref.shape[0]
    R = zero_buf.shape[0]

    @pl.when(pl.program_id(0) == 0)
    def _():
        zero_buf[...] = jnp.zeros_like(zero_buf)

        def tail_copy(e):
            start = pl.multiple_of(tail_ref[e], R)
            return pltpu.make_async_copy(zero_buf, xs_hbm.at[pl.ds(start, R), :], sem)

        def start_tail(e, c):
            @pl.when(tail_ref[e] >= 0)
            def _():
                tail_copy(e).start()
            return c

        def wait_tail(e, c):
            @pl.when(tail_ref[e] >= 0)
            def _():
                tail_copy(e).wait()
            return c

        lax.fori_loop(0, N_EXPERTS, start_tail, 0)
        lax.fori_loop(0, N_EXPERTS, wait_tail, 0)

    for k in range(TOP_K):
        def start_row(r, c, k=k):
            d = dest_ref[0, 0, k * tm + r]
            pltpu.make_async_copy(h_ref.at[pl.ds(r, 1), :], xs_hbm.at[pl.ds(d, 1), :], sem).start()
            return c

        lax.fori_loop(0, tm, start_row, 0, unroll=DMA_UNROLL)
    for k in range(TOP_K):
        pltpu.make_async_copy(h_ref, xs_hbm.at[pl.ds(0, tm), :], sem).wait()


def dispatch(dest_tiles, tail, h2, n_rows):
    T, D = h2.shape
    n_t = dest_tiles.shape[0]
    tm = T // n_t
    grid_spec = pltpu.PrefetchScalarGridSpec(
        num_scalar_prefetch=1,
        grid=(n_t,),
        in_specs=[
            pl.BlockSpec((1, 1, TOP_K * tm), lambda i, tail: (i, 0, 0), memory_space=pltpu.SMEM),
            pl.BlockSpec((tm, D), lambda i, tail: (i, 0)),
        ],
        out_specs=pl.BlockSpec(memory_space=pl.ANY),
        scratch_shapes=[pltpu.VMEM((EXPERT_ROWS, D), F32), pltpu.SemaphoreType.DMA],
    )
    return pl.pallas_call(
        _dispatch_kernel,
        grid_spec=grid_spec,
        out_shape=jax.ShapeDtypeStruct((n_rows, D), F32),
        compiler_params=_cparams(("arbitrary",)),
        name="dispatch",
    )(tail, dest_tiles, h2)


def _experts_kernel(blk_e_ref, blk_n_ref, x_ref, wgu_ref, wd_ref, o_ref):
    b = pl.program_id(0)
    F = wd_ref.shape[0]

    @pl.when(blk_n_ref[b] > 0)
    def _():
        gu = jnp.dot(x_ref[...].astype(BF16), wgu_ref[...], preferred_element_type=F32)
        act = (_silu(gu[:, :F]) * gu[:, F:]).astype(BF16)
        o_ref[...] = jnp.dot(act, wd_ref[...], preferred_element_type=F32)

    @pl.when(blk_n_ref[b] == 0)
    def _():
        o_ref[...] = jnp.zeros_like(o_ref)


def routed_experts(xs, blk_e, blk_n, w_gu, w_d):
    n_rows, D = xs.shape
    R = EXPERT_ROWS
    n_blk = n_rows // R
    F = w_d.shape[1]
    grid_spec = pltpu.PrefetchScalarGridSpec(
        num_scalar_prefetch=2,
        grid=(n_blk,),
        in_specs=[
            pl.BlockSpec((R, D), lambda b, be, bn: (jnp.where(bn[b] > 0, b, 0), 0)),
            pl.BlockSpec((None, D, 2 * F), lambda b, be, bn: (be[b], 0, 0)),
            pl.BlockSpec((None, F, D), lambda b, be, bn: (be[b], 0, 0)),
        ],
        out_specs=pl.BlockSpec((R, D), lambda b, be, bn: (b, 0)),
    )
    return pl.pallas_call(
        _experts_kernel,
        grid_spec=grid_spec,
        out_shape=jax.ShapeDtypeStruct((n_rows, D), F32),
        compiler_params=_cparams(("arbitrary",)),
        name="routed_experts",
    )(blk_e, blk_n, xs, w_gu, w_d)


COMBINE_ROWS = 128


def _combine_kernel(dest_ref, y_hbm, gate_ref, h_ref, x_ref, wgu_ref, wd_ref, gf_ref, lg_ref, lb_ref, o_ref,
                    buf, sem, *, alpha):
    tm = h_ref.shape[0]
    F = wd_ref.shape[0]

    for k in range(TOP_K):
        def start_row(r, c, k=k):
            d = dest_ref[0, 0, k * tm + r]
            pltpu.make_async_copy(y_hbm.at[pl.ds(d, 1), :], buf.at[k, pl.ds(r, 1), :], sem).start()
            return c

        lax.fori_loop(0, tm, start_row, 0, unroll=DMA_UNROLL)

    gu = jnp.dot(h_ref[...].astype(BF16), wgu_ref[...], preferred_element_type=F32)
    act = (_silu(gu[:, :F]) * gu[:, F:]).astype(BF16)
    ff = jnp.dot(act, wd_ref[...], preferred_element_type=F32)

    for k in range(TOP_K):
        pltpu.make_async_copy(y_hbm.at[pl.ds(0, tm), :], buf.at[k], sem).wait()
    gate = gate_ref[...]
    for k in range(TOP_K):
        ff = ff + gate[:, k:k + 1] * buf[k]
    o_ref[...] = _ln_rows(alpha * x_ref[...] + gf_ref[...] * ff, lg_ref[...], lb_ref[...])


def combine(dest_tiles, y_rows, gate, h2, x1, ws_gu, ws_d, g_f, ln_g, ln_b, seq, alpha):
    T, D = x1.shape
    F = ws_d.shape[0]
    n_t = dest_tiles.shape[0]
    tm = T // n_t
    per_b = seq // tm
    row = pl.BlockSpec((tm, D), lambda i: (i, 0))
    vec = pl.BlockSpec((1, D), lambda i: (0, 0))
    return pl.pallas_call(
        functools.partial(_combine_kernel, alpha=alpha),
        grid=(n_t,),
        in_specs=[
            pl.BlockSpec((1, 1, TOP_K * tm), lambda i: (i, 0, 0), memory_space=pltpu.SMEM),
            pl.BlockSpec(memory_space=pl.ANY),
            pl.BlockSpec((tm, TOP_K), lambda i: (i, 0)),
            row, row,
            pl.BlockSpec((D, 2 * F), lambda i: (0, 0)),
            pl.BlockSpec((F, D), lambda i: (0, 0)),
            pl.BlockSpec((None, 1, D), lambda i: (i // per_b, 0, 0)),
            vec, vec,
        ],
        out_specs=row,
        out_shape=jax.ShapeDtypeStruct((T, D), F32),
        scratch_shapes=[pltpu.VMEM((TOP_K, tm, D), F32), pltpu.SemaphoreType.DMA],
        compiler_params=_cparams(("arbitrary",)),
        name="combine",
    )(dest_tiles, y_rows, gate, h2, x1, ws_gu, ws_d, g_f, ln_g.reshape(1, D), ln_b.reshape(1, D))


def _dispatch_tables(eid, pos, counts_rows, T):
    R = EXPERT_ROWS
    n_rows = T * TOP_K + N_EXPERTS * R
    n_blk = n_rows // R
    experts = jnp.arange(N_EXPERTS, dtype=jnp.int32)
    counts = counts_rows.astype(jnp.int32).reshape(GROUP_SIZE, N_GROUPS).T.reshape(N_EXPERTS)
    padded = (counts + R - 1) // R * R
    pad_ends = jnp.cumsum(padded)
    pad_starts = pad_ends - padded
    dest = pos + jnp.sum(jnp.where(eid[..., None] == experts, pad_starts, 0), axis=-1)
    blk_start = jnp.arange(n_blk, dtype=jnp.int32) * R
    blk_e = jnp.minimum(jnp.sum(pad_ends[None, :] <= blk_start[:, None], axis=1), N_EXPERTS - 1).astype(jnp.int32)
    seg_end = jnp.sum(jnp.where(blk_e[:, None] == experts, pad_starts + counts, 0), axis=1)
    blk_n = jnp.clip(seg_end - blk_start, 0, R).astype(jnp.int32)
    tail = jnp.where(padded > 0, pad_ends - R, -1).astype(jnp.int32)
    return dest.astype(jnp.int32), blk_e, blk_n, tail, n_rows


def _tile_index(dest, tm):
    K, T = dest.shape
    return dest.reshape(K, T // tm, tm).transpose(1, 0, 2).reshape(T // tm, 1, K * tm)


def _trunk(x, mods, lams, p, depth):
    B, S, D = x.shape
    T = B * S
    aw = D // 2
    n_heads = aw // V_DIM
    alpha = (2 * depth) ** 0.25
    x = layer_norm_rows(x.reshape(T, D), p["emb_ln_g"], p["emb_ln_b"])
    for l in range(depth):
        mod = mods[l]
        sh_a, sc_a, g_a, sh_f, sc_f, g_f = (mod[:, i] for i in range(6))
        P = in_proj(x, sc_a, sh_a, p["w_in"][l], p["b_in"][l], S)
        lam_init = 0.8 - 0.6 * math.exp(-0.3 * l)
        attn = diff_attention(P, lams[l], p["attn_subln_g"][l], B, S, n_heads, 1.0 - lam_init)
        conv = conformer_conv(P, p["conv_w"][l], p["conv_b"][l], p["conv_ln_g"][l], p["conv_ln_b"][l],
                              B, S, 3 * aw // (D - aw), 3 * aw // (D - aw) + 1)
        x1, h2, logits_t = out_proj(attn, conv, x, p["w_out"][l], p["b_out"][l], g_a, p["ln1_g"][l], p["ln1_b"][l],
                                    sc_f, sh_f, p["w_router_t"][l], S, alpha)
        eid, gate, pos, counts_rows = route(logits_t, p["router_bias_rows"][l])
        dest, blk_e, blk_n, tail, n_rows = _dispatch_tables(eid, pos, counts_rows[:, 0], T)
        xs = dispatch(_tile_index(dest, _tile(S, DISPATCH_ROWS)), tail, h2, n_rows)
        y_rows = routed_experts(xs, blk_e, blk_n, p["w_gate_up"][l], p["w_down"][l])
        x = combine(_tile_index(dest, _tile(S, COMBINE_ROWS)), y_rows, gate.T, h2, x1, p["ws_gate_up"][l],
                    p["ws_down"][l], g_f, p["ln2_g"][l], p["ln2_b"][l], S, alpha)
    return x.reshape(B, S, D)


def kernel(x_prompt, x_sample, c_prompt, c_sample, emb_ln_g, emb_ln_b, w_ada, b_ada, w_in, b_in, lambda_q1, lambda_k1, lambda_q2, lambda_k2, attn_subln_g, conv_w, conv_b, conv_ln_g, conv_ln_b, w_out, b_out, ln1_g, ln1_b, w_router, router_bias, w_gate_up, w_down, ws_gate_up, ws_down, ln2_g, ln2_b):
    depth, D = w_in.shape[0], w_in.shape[1]
    Bp, Bs = x_prompt.shape[0], x_sample.shape[0]

    c_all = jnp.concatenate([c_prompt, c_sample], axis=0)
    pad = (-c_all.shape[0]) % 8
    c_all = jnp.pad(c_all, ((0, pad), (0, 0)))
    mod_all = ada_mod(c_all, w_ada, b_ada)
    mod_all = mod_all.reshape(depth, c_all.shape[0], 6, 1, D)
    lams = (jnp.exp(jnp.sum(lambda_q1 * lambda_k1, axis=-1)) - jnp.exp(jnp.sum(lambda_q2 * lambda_k2, axis=-1))
            + jnp.asarray([0.8 - 0.6 * math.exp(-0.3 * l) for l in range(depth)], F32))

    rows = jnp.arange(N_EXPERTS)
    perm = (rows % N_GROUPS) * GROUP_SIZE + rows // N_GROUPS
    params = dict(
        emb_ln_g=emb_ln_g, emb_ln_b=emb_ln_b,
        w_in=w_in.astype(BF16), b_in=b_in, attn_subln_g=attn_subln_g,
        conv_w=conv_w, conv_b=conv_b, conv_ln_g=conv_ln_g, conv_ln_b=conv_ln_b,
        w_out=w_out.astype(BF16), b_out=b_out, ln1_g=ln1_g, ln1_b=ln1_b,
        w_router_t=jnp.swapaxes(w_router, 1, 2)[:, perm, :],
        router_bias_rows=router_bias[:, perm].reshape(depth, N_EXPERTS, 1),
        w_gate_up=w_gate_up.astype(BF16), w_down=w_down.astype(BF16),
        ws_gate_up=ws_gate_up.astype(BF16), ws_down=ws_down.astype(BF16),
        ln2_g=ln2_g, ln2_b=ln2_b,
    )
    y_prompt = _trunk(x_prompt, mod_all[:, :Bp], lams, params, depth)
    y_sample = _trunk(x_sample, mod_all[:, Bp:Bp + Bs], lams, params, depth)
    return (y_prompt, y_sample)
```

```python
import functools
import math

import jax
import jax.numpy as jnp
from jax import lax
from jax.experimental import pallas as pl
from jax.experimental.pallas import tpu as pltpu

F32 = jnp.float32
BF16 = jnp.bfloat16

HEAD_DIM = 64
V_DIM = 2 * HEAD_DIM
CONV_TAPS = 31
CONV_HALF = CONV_TAPS // 2
CONV_HALO = 16
N_EXPERTS = 64
N_GROUPS = 8
GROUP_SIZE = N_EXPERTS // N_GROUPS
TOPK_GROUPS = 4
TOP_K = 8
ROUTED_SCALE = 2.5
LN_EPS = 1e-5

V7X_VMEM_BYTES = 64 * 1024 * 1024
VMEM_LIMIT = V7X_VMEM_BYTES - 12 * 1024 * 1024


def _cparams(sem):
    return pltpu.CompilerParams(dimension_semantics=sem, vmem_limit_bytes=VMEM_LIMIT)


def _tile(n, pref, align=8):
    if n <= pref:
        return n
    for t in range(pref - pref % align, 0, -align):
        if n % t == 0:
            return t
    raise ValueError((n, pref, align))


def _ln_rows(v, g, b):
    mu = jnp.mean(v, axis=-1, keepdims=True)
    d = v - mu
    var = jnp.mean(d * d, axis=-1, keepdims=True)
    return d * lax.rsqrt(var + LN_EPS) * g + b


def _silu(v):
    return v * jax.nn.sigmoid(v)


def _mod_kernel(c_ref, w_ref, b_ref, o_ref):
    a = _silu(c_ref[...]).astype(BF16)
    o_ref[...] = jnp.dot(a, w_ref[...].astype(BF16), preferred_element_type=F32) + b_ref[...]


def ada_mod(c, w_ada, b_ada):
    L, D, N = w_ada.shape
    Bt = c.shape[0]
    tn = _tile(N, 1024, 128)
    return pl.pallas_call(
        _mod_kernel,
        grid=(L, N // tn),
        in_specs=[
            pl.BlockSpec((Bt, D), lambda l, j: (0, 0)),
            pl.BlockSpec((None, D, tn), lambda l, j: (l, 0, j)),
            pl.BlockSpec((None, 1, tn), lambda l, j: (l, 0, j)),
        ],
        out_specs=pl.BlockSpec((None, Bt, tn), lambda l, j: (l, 0, j)),
        out_shape=jax.ShapeDtypeStruct((L, Bt, N), F32),
        compiler_params=_cparams(("parallel", "parallel")),
        name="ada_mod",
    )(c, w_ada, b_ada.reshape(L, 1, N))


def _ln_kernel(x_ref, g_ref, b_ref, o_ref):
    o_ref[...] = _ln_rows(x_ref[...], g_ref[...], b_ref[...])


def layer_norm_rows(x, g, b):
    T, D = x.shape
    tm = _tile(T, 512)
    return pl.pallas_call(
        _ln_kernel,
        grid=(T // tm,),
        in_specs=[
            pl.BlockSpec((tm, D), lambda i: (i, 0)),
            pl.BlockSpec((1, D), lambda i: (0, 0)),
            pl.BlockSpec((1, D), lambda i: (0, 0)),
        ],
        out_specs=pl.BlockSpec((tm, D), lambda i: (i, 0)),
        out_shape=jax.ShapeDtypeStruct((T, D), F32),
        compiler_params=_cparams(("parallel",)),
        name="emb_ln",
    )(x, g.reshape(1, D), b.reshape(1, D))


def _inproj_kernel(x_ref, sc_ref, sh_ref, w_ref, b_ref, o_ref, h_scr):
    @pl.when(pl.program_id(1) == 0)
    def _():
        h_scr[...] = (x_ref[...] * (1.0 + sc_ref[...]) + sh_ref[...]).astype(BF16)

    acc = jnp.dot(h_scr[...], w_ref[...], preferred_element_type=F32)
    o_ref[...] = (acc + b_ref[...]).astype(o_ref.dtype)


def in_proj(x, sc, sh, w, layer, b, seq):
    T, D = x.shape
    N = w.shape[2]
    tm = _tile(seq, 512)
    tn = _tile(N, 1024, 128)
    per_b = seq // tm
    return pl.pallas_call(
        _inproj_kernel,
        grid=(T // tm, N // tn),
        in_specs=[
            pl.BlockSpec((tm, D), lambda i, j: (i, 0)),
            pl.BlockSpec((None, 1, D), lambda i, j: (i // per_b, 0, 0)),
            pl.BlockSpec((None, 1, D), lambda i, j: (i // per_b, 0, 0)),
            pl.BlockSpec((None, D, tn), lambda i, j: (layer, 0, j)),
            pl.BlockSpec((1, tn), lambda i, j: (0, j)),
        ],
        out_specs=pl.BlockSpec((tm, tn), lambda i, j: (i, j)),
        out_shape=jax.ShapeDtypeStruct((T, N), BF16),
        scratch_shapes=[pltpu.VMEM((tm, D), BF16)],
        compiler_params=_cparams(("parallel", "arbitrary")),
        name="in_proj",
    )(x, sc, sh, w, b.reshape(1, N))


ATTN_TILE = 256
POS_SPLIT = 256
EXP_CHUNK = 512


def _attn_kernel(lam_ref, q_ref, k_ref, v_ref, g_ref, o_ref, qe_scr, ke_scr, ve_scr, s_scr, p_scr,
                 *, n_heads, out_scale):
    h = pl.program_id(1)
    qi = pl.program_id(2)
    tq = q_ref.shape[0]
    S = k_ref.shape[0]
    nk = S // tq
    lam = lam_ref[0, 0]
    slope = jnp.exp2(-(jnp.full((1, 1), h, jnp.int32) + 1).astype(F32) * (8.0 / n_heads))
    nt_dims = (((1,), (1,)), ((), ()))

    def features(pos, a, b):
        lane = lax.broadcasted_iota(jnp.int32, pos.shape, 1)
        hi = (pos // POS_SPLIT).astype(F32) * (slope * POS_SPLIT)
        lo = (pos % POS_SPLIT).astype(F32) * slope
        f = jnp.where(lane == HEAD_DIM + a[0], hi, 0.0)
        f = jnp.where(lane == HEAD_DIM + a[1], lo, f)
        return jnp.where((lane == HEAD_DIM + b[0]) | (lane == HEAD_DIM + b[1]), 1.0, f)

    @pl.when(qi == 0)
    def _():
        kpos = lax.broadcasted_iota(jnp.int32, (S, V_DIM), 0)
        lane = lax.broadcasted_iota(jnp.int32, (S, V_DIM), 1)
        kfeat = features(kpos, (2, 3), (0, 1))
        kf = k_ref[...].astype(F32)
        ke_scr[0] = jnp.where(lane < HEAD_DIM, kf, kfeat).astype(BF16)
        ke_scr[1] = jnp.where(lane < HEAD_DIM, pltpu.roll(kf, HEAD_DIM, axis=1), kfeat).astype(BF16)
        ve_scr[:, 0:V_DIM] = v_ref[...]
        ve_scr[:, V_DIM:] = jnp.where(lane == 0, 1.0, 0.0).astype(BF16)

    qpos = qi * tq + lax.broadcasted_iota(jnp.int32, (tq, V_DIM), 0)
    lane = lax.broadcasted_iota(jnp.int32, (tq, V_DIM), 1)
    qfeat = features(qpos, (0, 1), (2, 3))
    qfeat = jnp.where(lane < HEAD_DIM + 2, -qfeat, qfeat)
    qf = q_ref[...].astype(F32) * (HEAD_DIM ** -0.5)
    for j in range(2):
        qj = qf if j == 0 else pltpu.roll(qf, HEAD_DIM, axis=1)
        qe_scr[0, j] = jnp.where(lane < HEAD_DIM, qj, qfeat).astype(BF16)
        qe_scr[1, j] = jnp.where(lane < HEAD_DIM, qj, -qfeat).astype(BF16)
        qe_scr[2, j] = jnp.where(lane < HEAD_DIM, qj, 0.0).astype(BF16)

    rel = (lax.broadcasted_iota(jnp.int32, (tq, tq), 0) - lax.broadcasted_iota(jnp.int32, (tq, tq), 1))
    diag_bias = -slope * jnp.abs(rel).astype(F32)

    def scores(j):
        for r in range(nk):
            c = qi + r
            wrapped = c >= nk
            c = jnp.where(wrapped, c - nk, c)
            off = pl.multiple_of(c * tq, tq)
            variant = 2 if r == 0 else jnp.where(wrapped, 0, 1)
            s = lax.dot_general(qe_scr[variant, j], ke_scr[j, pl.ds(off, tq), :], nt_dims,
                                preferred_element_type=F32)
            s_scr[j, :, pl.ds(off, tq)] = s + diag_bias if r == 0 else s

    def row_max(j):
        return jnp.max(s_scr[j], axis=1, keepdims=True)

    def exps(j, m):
        w = min(EXP_CHUNK, S)
        for c in range(S // w):
            p_scr[j, :, c * w:(c + 1) * w] = jnp.exp(s_scr[j, :, c * w:(c + 1) * w] - m).astype(BF16)

    def weighted_values(j):
        acc = jnp.dot(p_scr[j], ve_scr[...], preferred_element_type=F32)
        return acc[:, 0:V_DIM] / acc[:, V_DIM:V_DIM + 1]

    scores(0)
    scores(1)
    m0 = row_max(0)
    exps(0, m0)
    m1 = row_max(1)
    o0 = weighted_values(0)
    exps(1, m1)
    o1 = weighted_values(1)

    o = o0 - lam * o1
    o = o * lax.rsqrt(jnp.mean(o * o, axis=-1, keepdims=True) + LN_EPS) * g_ref[...] * out_scale
    o_ref[...] = o.astype(o_ref.dtype)


def diff_attention(P, lam, subln_g, batch, seq, n_heads, out_scale):
    T = P.shape[0]
    tq = _tile(seq, ATTN_TILE)
    nq = seq // tq
    return pl.pallas_call(
        functools.partial(_attn_kernel, n_heads=n_heads, out_scale=out_scale),
        grid=(batch, n_heads, nq),
        in_specs=[
            pl.BlockSpec(memory_space=pltpu.SMEM),
            pl.BlockSpec((tq, V_DIM), lambda b, h, i: (b * nq + i, h)),
            pl.BlockSpec((seq, V_DIM), lambda b, h, i: (b, n_heads + h)),
            pl.BlockSpec((seq, V_DIM), lambda b, h, i: (b, 2 * n_heads + h)),
            pl.BlockSpec((1, V_DIM), lambda b, h, i: (0, 0)),
        ],
        out_specs=pl.BlockSpec((tq, V_DIM), lambda b, h, i: (b * nq + i, h)),
        out_shape=jax.ShapeDtypeStruct((T, n_heads * V_DIM), BF16),
        scratch_shapes=[
            pltpu.VMEM((3, 2, tq, V_DIM), BF16),
            pltpu.VMEM((2, seq, V_DIM), BF16),
            pltpu.VMEM((seq, 2 * V_DIM), BF16),
            pltpu.VMEM((2, tq, seq), F32),
            pltpu.VMEM((2, tq, seq), BF16),
        ],
        compiler_params=_cparams(("parallel", "parallel", "arbitrary")),
        name="diff_attn",
    )(lam.reshape(1, 1), P, P, P, subln_g.reshape(1, V_DIM))


CONV_ROW_CHUNK = 32
CONV_TILE = 256
LANES = 128
SUBLANES = 8


def _conv_kernel(ap_ref, ac_ref, an_ref, gp_ref, gc_ref, gn_ref, w_ref, cb_ref, lg_ref, lb_ref,
                 o_ref, u_scr, y_scr):
    i = pl.program_id(1)
    ts, C = o_ref.shape

    def glu(a_ref, g_ref):
        return a_ref[...].astype(F32) * jax.nn.sigmoid(g_ref[...].astype(F32))

    rows = ts + 2 * CONV_HALO
    u_scr[0, 0:CONV_HALO, :] = jnp.where(i > 0, glu(ap_ref, gp_ref), 0.0)
    u_scr[0, CONV_HALO:CONV_HALO + ts, :] = glu(ac_ref, gc_ref)
    u_scr[0, CONV_HALO + ts:, :] = jnp.where(i < pl.num_programs(1) - 1, glu(an_ref, gn_ref), 0.0)
    for s in range(1, SUBLANES):
        u_scr[s, 0:rows - SUBLANES, :] = u_scr[0, s:s + rows - SUBLANES, :]

    first = CONV_HALO - CONV_HALF
    for c0 in range(0, C, LANES):
        wv = w_ref[:, c0:c0 + LANES]
        cb = cb_ref[:, c0:c0 + LANES]
        for r0 in range(0, ts, CONV_ROW_CHUNK):
            acc = jnp.zeros((CONV_ROW_CHUNK, LANES), F32)
            for t in range(CONV_TAPS):
                s = (first + t) % SUBLANES
                base = r0 + first + t - s
                acc = acc + u_scr[s, base:base + CONV_ROW_CHUNK, c0:c0 + LANES] * wv[t:t + 1, :]
            y_scr[r0:r0 + CONV_ROW_CHUNK, c0:c0 + LANES] = acc + cb

    y = _ln_rows(y_scr[...], lg_ref[...], lb_ref[...])
    o_ref[...] = _silu(y).astype(o_ref.dtype)


def conformer_conv(P, conv_w, conv_b, ln_g, ln_b, batch, seq, col_a, col_g):
    T = P.shape[0]
    C = conv_w.shape[1]
    ts = _tile(seq, CONV_TILE)
    ns = seq // ts
    hb = ts // CONV_HALO
    last_halo = T // CONV_HALO - 1

    def cur(col):
        return pl.BlockSpec((ts, C), lambda b, i: (b * ns + i, col))

    def prev(col):
        return pl.BlockSpec((CONV_HALO, C), lambda b, i: (jnp.maximum((b * ns + i) * hb - 1, 0), col))

    def nxt(col):
        return pl.BlockSpec((CONV_HALO, C), lambda b, i: (jnp.minimum((b * ns + i + 1) * hb, last_halo), col))

    vec = pl.BlockSpec((1, C), lambda b, i: (0, 0))
    return pl.pallas_call(
        _conv_kernel,
        grid=(batch, ns),
        in_specs=[prev(col_a), cur(col_a), nxt(col_a), prev(col_g), cur(col_g), nxt(col_g),
                  pl.BlockSpec((CONV_TAPS, C), lambda b, i: (0, 0)), vec, vec, vec],
        out_specs=pl.BlockSpec((ts, C), lambda b, i: (b * ns + i, 0)),
        out_shape=jax.ShapeDtypeStruct((T, C), BF16),
        scratch_shapes=[pltpu.VMEM((SUBLANES, ts + 2 * CONV_HALO, C), F32), pltpu.VMEM((ts, C), F32)],
        compiler_params=_cparams(("parallel", "arbitrary")),
        name="conformer_conv",
    )(P, P, P, P, P, P, conv_w, conv_b.reshape(1, C), ln_g.reshape(1, C), ln_b.reshape(1, C))


def _outproj_kernel(at_ref, cv_ref, x_ref, w_ref, b_ref, ga_ref, l1g_ref, l1b_ref, sc_ref, sh_ref, wr_ref,
                    x1_ref, h2_ref, lg_ref, *, alpha):
    aw = at_ref.shape[1]
    mix = jnp.dot(at_ref[...], w_ref[0:aw, :], preferred_element_type=F32)
    mix = mix + jnp.dot(cv_ref[...], w_ref[aw:, :], preferred_element_type=F32) + b_ref[...]
    x1 = _ln_rows(alpha * x_ref[...] + ga_ref[...] * mix, l1g_ref[...], l1b_ref[...])
    x1_ref[...] = x1
    h2 = x1 * (1.0 + sc_ref[...]) + sh_ref[...]
    h2_ref[...] = h2
    lg_ref[...] = lax.dot_general(wr_ref[...], h2, (((1,), (1,)), ((), ())),
                                  precision=lax.Precision.HIGHEST, preferred_element_type=F32)


def out_proj(attn, conv, x, w_out, layer, b_out, g_a, ln_g, ln_b, sc_f, sh_f, w_router_t, seq, alpha):
    T, D = x.shape
    aw = attn.shape[1]
    cw = conv.shape[1]
    E = w_router_t.shape[0]
    tm = _tile(seq, 256)
    per_b = seq // tm
    vec = pl.BlockSpec((1, D), lambda i: (0, 0))
    bvec = pl.BlockSpec((None, 1, D), lambda i: (i // per_b, 0, 0))
    row = pl.BlockSpec((tm, D), lambda i: (i, 0))
    return pl.pallas_call(
        functools.partial(_outproj_kernel, alpha=alpha),
        grid=(T // tm,),
        in_specs=[
            pl.BlockSpec((tm, aw), lambda i: (i, 0)),
            pl.BlockSpec((tm, cw), lambda i: (i, 0)),
            row,
            pl.BlockSpec((None, aw + cw, D), lambda i: (layer, 0, 0)),
            vec, bvec, vec, vec, bvec, bvec,
            pl.BlockSpec((E, D), lambda i: (0, 0)),
        ],
        out_specs=[row, row, pl.BlockSpec((E, tm), lambda i: (0, i))],
        out_shape=[jax.ShapeDtypeStruct((T, D), F32), jax.ShapeDtypeStruct((T, D), F32),
                   jax.ShapeDtypeStruct((E, T), F32)],
        compiler_params=_cparams(("parallel",)),
        name="out_proj",
    )(attn, conv, x, w_out, b_out.reshape(1, D), g_a, ln_g.reshape(1, D), ln_b.reshape(1, D), sc_f, sh_f,
      w_router_t)


def _route_kernel(lg_ref, rb_ref, eid_ref, gate_ref, pos_ref, cnt_ref, carry):
    step = pl.program_id(0)
    tn = lg_ref.shape[1]

    @pl.when(step == 0)
    def _():
        carry[...] = jnp.zeros_like(carry)

    grp = lax.broadcasted_iota(jnp.int32, (N_GROUPS, tn), 0)
    scores, sel, ids = [], [], []
    for i in range(GROUP_SIZE):
        sc = jax.nn.sigmoid(lg_ref[i * N_GROUPS:(i + 1) * N_GROUPS, :])
        scores.append(sc)
        sel.append(sc + rb_ref[i * N_GROUPS:(i + 1) * N_GROUPS, :])
        ids.append(grp * GROUP_SIZE + i)

    top1 = sel[0]
    top2 = jnp.full_like(top1, -jnp.inf)
    for i in range(1, GROUP_SIZE):
        top2 = jnp.maximum(top2, jnp.minimum(top1, sel[i]))
        top1 = jnp.maximum(top1, sel[i])
    gscore = top1 + top2

    rank = jnp.zeros((N_GROUPS, tn), jnp.int32)
    for r in range(1, N_GROUPS):
        other = pltpu.roll(gscore, r, axis=0)
        other_idx = pltpu.roll(grp, r, axis=0)
        beats = (other > gscore) | ((other == gscore) & (other_idx < grp))
        rank = rank + beats.astype(jnp.int32)
    gmask = rank < TOPK_GROUPS

    masked = [jnp.where(gmask, s, -jnp.inf) for s in sel]
    onehots, gates = [], []
    for _ in range(TOP_K):
        m = masked[0]
        for i in range(1, GROUP_SIZE):
            m = jnp.maximum(m, masked[i])
        m = jnp.max(m, axis=0, keepdims=True)
        cand = jnp.where(masked[0] == m, ids[0], N_EXPERTS)
        for i in range(1, GROUP_SIZE):
            cand = jnp.minimum(cand, jnp.where(masked[i] == m, ids[i], N_EXPERTS))
        chosen = jnp.min(cand, axis=0, keepdims=True)
        oh = [ids[i] == chosen for i in range(GROUP_SIZE)]
        g = jnp.where(oh[0], scores[0], 0.0)
        for i in range(1, GROUP_SIZE):
            g = g + jnp.where(oh[i], scores[i], 0.0)
        gates.append(jnp.sum(g, axis=0, keepdims=True))
        masked = [jnp.where(oh[i], -jnp.inf, masked[i]) for i in range(GROUP_SIZE)]
        onehots.append(oh)
        eid_ref[pl.ds(len(onehots) - 1, 1), :] = chosen

    denom = gates[0]
    for kk in range(1, TOP_K):
        denom = denom + gates[kk]
    for kk in range(TOP_K):
        gate_ref[pl.ds(kk, 1), :] = gates[kk] / denom * ROUTED_SCALE

    chosen_any = []
    for i in range(GROUP_SIZE):
        a = onehots[0][i]
        for kk in range(1, TOP_K):
            a = a | onehots[kk][i]
        chosen_any.append(a.astype(F32))
    mask = jnp.concatenate(chosen_any, axis=0).astype(BF16)
    tri = (lax.broadcasted_iota(jnp.int32, (tn, tn), 0) <= lax.broadcasted_iota(jnp.int32, (tn, tn), 1))
    csum = jnp.dot(mask, tri.astype(BF16), preferred_element_type=F32) + carry[...]
    for kk in range(TOP_K):
        p = jnp.zeros((N_GROUPS, tn), F32)
        for i in range(GROUP_SIZE):
            p = p + jnp.where(onehots[kk][i], csum[i * N_GROUPS:(i + 1) * N_GROUPS, :], 0.0)
        pos_ref[pl.ds(kk, 1), :] = (jnp.sum(p, axis=0, keepdims=True) - 1.0).astype(jnp.int32)
    total = csum[:, tn - 1:tn]
    carry[...] = total
    cnt_ref[...] = jnp.broadcast_to(total, cnt_ref.shape)


def route(logits_t, bias_rows):
    E, T = logits_t.shape
    tn = _tile(T, 512)
    slot = pl.BlockSpec((TOP_K, tn), lambda i: (0, i))
    return pl.pallas_call(
        _route_kernel,
        grid=(T // tn,),
        in_specs=[pl.BlockSpec((E, tn), lambda i: (0, i)), pl.BlockSpec((E, 1), lambda i: (0, 0))],
        out_specs=[slot, slot, slot, pl.BlockSpec((E, LANES), lambda i: (0, 0))],
        out_shape=[jax.ShapeDtypeStruct((TOP_K, T), jnp.int32), jax.ShapeDtypeStruct((TOP_K, T), F32),
                   jax.ShapeDtypeStruct((TOP_K, T), jnp.int32), jax.ShapeDtypeStruct((E, LANES), F32)],
        scratch_shapes=[pltpu.VMEM((E, 1), F32)],
        compiler_params=_cparams(("arbitrary",)),
        name="route",
    )(logits_t, bias_rows)


EXPERT_ROWS = 512
DISPATCH_ROWS = 256


def _dispatch_kernel(tail_ref, dest_ref, h_ref, xs_hbm, zero_buf, sem):
    tm = h_ref.shape[0]
    R = zero_buf.shape[0]

    @pl.when(pl.program_id(0) == 0)
    def _():
        zero_buf[...] = jnp.zeros_like(zero_buf)

        def tail_copy(e):
            start = pl.multiple_of(tail_ref[e], R)
            return pltpu.make_async_copy(zero_buf, xs_hbm.at[pl.ds(start, R), :], sem)

        def start_tail(e, c):
            @pl.when(tail_ref[e] >= 0)
            def _():
                tail_copy(e).start()
            return c

        def wait_tail(e, c):
            @pl.when(tail_ref[e] >= 0)
            def _():
                tail_copy(e).wait()
            return c

        lax.fori_loop(0, 2 * N_EXPERTS, start_tail, 0)
        lax.fori_loop(0, 2 * N_EXPERTS, wait_tail, 0)

    for n in range(TOP_K * tm):
        r = n % tm
        pltpu.make_async_copy(h_ref.at[pl.ds(r, 1), :], xs_hbm.at[pl.ds(dest_ref[0, 0, n], 1), :], sem).start()
    for k in range(TOP_K):
        pltpu.make_async_copy(h_ref, xs_hbm.at[pl.ds(0, tm), :], sem).wait()


def dispatch(dest_tiles, tail, h2, n_rows):
    T, D = h2.shape
    n_t = dest_tiles.shape[0]
    tm = T // n_t
    grid_spec = pltpu.PrefetchScalarGridSpec(
        num_scalar_prefetch=1,
        grid=(n_t,),
        in_specs=[
            pl.BlockSpec((1, 1, TOP_K * tm), lambda i, tail: (i, 0, 0), memory_space=pltpu.SMEM),
            pl.BlockSpec((tm, D), lambda i, tail: (i, 0)),
        ],
        out_specs=pl.BlockSpec(memory_space=pl.ANY),
        scratch_shapes=[pltpu.VMEM((EXPERT_ROWS, D), F32), pltpu.SemaphoreType.DMA],
    )
    return pl.pallas_call(
        _dispatch_kernel,
        grid_spec=grid_spec,
        out_shape=jax.ShapeDtypeStruct((n_rows, D), F32),
        compiler_params=_cparams(("arbitrary",)),
        name="dispatch",
    )(tail, dest_tiles, h2)


def _experts_kernel(blk_e_ref, blk_n_ref, x_ref, wgu_ref, wd_ref, o_ref):
    b = pl.program_id(0)
    F = wd_ref.shape[0]

    @pl.when(blk_n_ref[b] > 0)
    def _():
        gu = jnp.dot(x_ref[...].astype(BF16), wgu_ref[...], preferred_element_type=F32)
        act = (_silu(gu[:, :F]) * gu[:, F:]).astype(BF16)
        o_ref[...] = jnp.dot(act, wd_ref[...], preferred_element_type=F32)

    @pl.when(blk_n_ref[b] == 0)
    def _():
        o_ref[...] = jnp.zeros_like(o_ref)


def routed_experts(xs, blk_e, blk_n, w_gu, w_d, layer):
    n_rows, D = xs.shape
    R = EXPERT_ROWS
    n_blk = n_rows // R
    F = w_d.shape[2]
    grid_spec = pltpu.PrefetchScalarGridSpec(
        num_scalar_prefetch=2,
        grid=(n_blk,),
        in_specs=[
            pl.BlockSpec((R, D), lambda b, be, bn: (jnp.where(bn[b] > 0, b, 0), 0)),
            pl.BlockSpec((None, None, D, 2 * F), lambda b, be, bn: (layer, be[b], 0, 0)),
            pl.BlockSpec((None, None, F, D), lambda b, be, bn: (layer, be[b], 0, 0)),
        ],
        out_specs=pl.BlockSpec((R, D), lambda b, be, bn: (b, 0)),
    )
    return pl.pallas_call(
        _experts_kernel,
        grid_spec=grid_spec,
        out_shape=jax.ShapeDtypeStruct((n_rows, D), F32),
        compiler_params=_cparams(("arbitrary",)),
        name="routed_experts",
    )(blk_e, blk_n, xs, w_gu, w_d)


COMBINE_ROWS = 128


def _combine_kernel(dest_ref, next_dest_ref, y_hbm, gate_ref, h_ref, x_ref, wgu_ref, wd_ref, gf_ref, lg_ref, lb_ref,
                    o_ref, buf, sem, *, alpha):
    i = pl.program_id(0)
    last = pl.num_programs(0) - 1
    tm = h_ref.shape[0] // 2
    F = wd_ref.shape[0]

    def gather(idx_ref, half, slots):
        for k in slots:
            for r in range(tm):
                src = idx_ref[0, 0, (half * TOP_K + k) * tm + r]
                pltpu.make_async_copy(y_hbm.at[pl.ds(src, 1), :], buf.at[half, k, pl.ds(r, 1), :],
                                      sem.at[half]).start()

    def wait_gather(half):
        for k in range(TOP_K):
            pltpu.make_async_copy(y_hbm.at[pl.ds(0, tm), :], buf.at[half, k], sem.at[half]).wait()

    def finish(half, idx_ref, other):
        rows = slice(half * tm, (half + 1) * tm)
        gather(idx_ref, other, (0, 1))
        gu = jnp.dot(h_ref[rows, :].astype(BF16), wgu_ref[...], preferred_element_type=F32)
        gather(idx_ref, other, (2, 3))
        act = (_silu(gu[:, :F]) * gu[:, F:]).astype(BF16)
        ff = jnp.dot(act, wd_ref[...], preferred_element_type=F32)
        gather(idx_ref, other, (4, 5))
        wait_gather(half)
        gate = gate_ref[rows, :]
        for k in range(TOP_K):
            ff = ff + gate[:, k:k + 1] * buf[half, k]
        gather(idx_ref, other, (6, 7))
        o_ref[rows, :] = _ln_rows(alpha * x_ref[rows, :] + gf_ref[...] * ff, lg_ref[...], lb_ref[...])

    @pl.when(i == 0)
    def _():
        gather(dest_ref, 0, range(TOP_K))

    finish(0, dest_ref, 1)
    finish(1, next_dest_ref, 0)

    @pl.when(i == last)
    def _():
        wait_gather(0)


def combine(dest_tiles, y_rows, gate, h2, x1, ws_gu, ws_d, layer, g_f, ln_g, ln_b, seq, alpha):
    T, D = x1.shape
    F = ws_d.shape[1]
    half = T // dest_tiles.shape[0]
    tm = 2 * half
    n_t = T // tm
    dest_steps = dest_tiles.reshape(n_t, 1, 2 * TOP_K * half)
    per_b = seq // tm
    row = pl.BlockSpec((tm, D), lambda i: (i, 0))
    vec = pl.BlockSpec((1, D), lambda i: (0, 0))
    return pl.pallas_call(
        functools.partial(_combine_kernel, alpha=alpha),
        grid=(n_t,),
        in_specs=[
            pl.BlockSpec((1, 1, 2 * TOP_K * half), lambda i: (i, 0, 0), memory_space=pltpu.SMEM),
            pl.BlockSpec((1, 1, 2 * TOP_K * half), lambda i: (jnp.minimum(i + 1, n_t - 1), 0, 0),
                         memory_space=pltpu.SMEM),
            pl.BlockSpec(memory_space=pl.ANY),
            pl.BlockSpec((tm, TOP_K), lambda i: (i, 0)),
            row, row,
            pl.BlockSpec((None, D, 2 * F), lambda i: (layer, 0, 0)),
            pl.BlockSpec((None, F, D), lambda i: (layer, 0, 0)),
            pl.BlockSpec((None, 1, D), lambda i: (i // per_b, 0, 0)),
            vec, vec,
        ],
        out_specs=row,
        out_shape=jax.ShapeDtypeStruct((T, D), F32),
        scratch_shapes=[pltpu.VMEM((2, TOP_K, half, D), F32), pltpu.SemaphoreType.DMA((2,))],
        compiler_params=_cparams(("arbitrary",)),
        name="combine",
    )(dest_steps, dest_steps, y_rows, gate, h2, x1, ws_gu, ws_d, g_f, ln_g.reshape(1, D), ln_b.reshape(1, D))


def _dispatch_tables(eid, pos, counts_rows, T):
    R = EXPERT_ROWS
    n_rows = T * TOP_K + N_EXPERTS * R
    n_blk = n_rows // R
    experts = jnp.arange(N_EXPERTS, dtype=jnp.int32)
    counts = counts_rows.astype(jnp.int32).reshape(GROUP_SIZE, N_GROUPS).T.reshape(N_EXPERTS)
    padded = (counts + R - 1) // R * R
    pad_ends = jnp.cumsum(padded)
    pad_starts = pad_ends - padded
    dest = pos + jnp.sum(jnp.where(eid[..., None] == experts, pad_starts, 0), axis=-1)
    blk_start = jnp.arange(n_blk, dtype=jnp.int32) * R
    blk_e = jnp.minimum(jnp.sum(pad_ends[None, :] <= blk_start[:, None], axis=1), N_EXPERTS - 1).astype(jnp.int32)
    seg_end = jnp.sum(jnp.where(blk_e[:, None] == experts, pad_starts + counts, 0), axis=1)
    blk_n = jnp.clip(seg_end - blk_start, 0, R).astype(jnp.int32)
    unused = pad_ends[-1] + experts * R
    tail = jnp.concatenate([jnp.where(padded > 0, pad_ends - R, -1),
                            jnp.where(unused < n_rows, unused, -1)]).astype(jnp.int32)
    return dest.astype(jnp.int32), blk_e, blk_n, tail, n_rows


def _tile_index(dest, tm):
    K, T = dest.shape
    return dest.reshape(K, T // tm, tm).transpose(1, 0, 2).reshape(T // tm, 1, K * tm)


def _trunk(x, mods, lams, p, depth):
    B, S, D = x.shape
    T = B * S
    aw = D // 2
    n_heads = aw // V_DIM
    alpha = (2 * depth) ** 0.25
    x = layer_norm_rows(x.reshape(T, D), p["emb_ln_g"], p["emb_ln_b"])
    for l in range(depth):
        mod = mods[l]
        sh_a, sc_a, g_a, sh_f, sc_f, g_f = (mod[:, i] for i in range(6))
        P = in_proj(x, sc_a, sh_a, p["w_in"], l, p["b_in"][l], S)
        lam_init = 0.8 - 0.6 * math.exp(-0.3 * l)
        attn = diff_attention(P, lams[l], p["attn_subln_g"][l], B, S, n_heads, 1.0 - lam_init)
        conv = conformer_conv(P, p["conv_w"][l], p["conv_b"][l], p["conv_ln_g"][l], p["conv_ln_b"][l],
                              B, S, 3 * aw // (D - aw), 3 * aw // (D - aw) + 1)
        x1, h2, logits_t = out_proj(attn, conv, x, p["w_out"], l, p["b_out"][l], g_a, p["ln1_g"][l], p["ln1_b"][l],
                                    sc_f, sh_f, p["w_router_t"][l], S, alpha)
        eid, gate, pos, counts_rows = route(logits_t, p["router_bias_rows"][l])
        dest, blk_e, blk_n, tail, n_rows = _dispatch_tables(eid, pos, counts_rows[:, 0], T)
        xs = dispatch(_tile_index(dest, _tile(S, DISPATCH_ROWS)), tail, h2, n_rows)
        y_rows = routed_experts(xs, blk_e, blk_n, p["w_gate_up"], p["w_down"], l)
        x = combine(_tile_index(dest, _tile(S, COMBINE_ROWS)), y_rows, gate.T, h2, x1, p["ws_gate_up"],
                    p["ws_down"], l, g_f, p["ln2_g"][l], p["ln2_b"][l], S, alpha)
    return x.reshape(B, S, D)


def kernel(x_prompt, x_sample, c_prompt, c_sample, emb_ln_g, emb_ln_b, w_ada, b_ada, w_in, b_in, lambda_q1, lambda_k1, lambda_q2, lambda_k2, attn_subln_g, conv_w, conv_b, conv_ln_g, conv_ln_b, w_out, b_out, ln1_g, ln1_b, w_router, router_bias, w_gate_up, w_down, ws_gate_up, ws_down, ln2_g, ln2_b):
    depth, D = w_in.shape[0], w_in.shape[1]
    Bp, Bs = x_prompt.shape[0], x_sample.shape[0]

    c_all = jnp.concatenate([c_prompt, c_sample], axis=0)
    pad = (-c_all.shape[0]) % 8
    c_all = jnp.pad(c_all, ((0, pad), (0, 0)))
    mod_all = ada_mod(c_all, w_ada, b_ada)
    mod_all = mod_all.reshape(depth, c_all.shape[0], 6, 1, D)
    lams = (jnp.exp(jnp.sum(lambda_q1 * lambda_k1, axis=-1)) - jnp.exp(jnp.sum(lambda_q2 * lambda_k2, axis=-1))
            + jnp.asarray([0.8 - 0.6 * math.exp(-0.3 * l) for l in range(depth)], F32))

    rows = jnp.arange(N_EXPERTS)
    perm = (rows % N_GROUPS) * GROUP_SIZE + rows // N_GROUPS
    params = dict(
        emb_ln_g=emb_ln_g, emb_ln_b=emb_ln_b,
        w_in=w_in.astype(BF16), b_in=b_in, attn_subln_g=attn_subln_g,
        conv_w=conv_w, conv_b=conv_b, conv_ln_g=conv_ln_g, conv_ln_b=conv_ln_b,
        w_out=w_out.astype(BF16), b_out=b_out, ln1_g=ln1_g, ln1_b=ln1_b,
        w_router_t=jnp.swapaxes(w_router, 1, 2)[:, perm, :],
        router_bias_rows=router_bias[:, perm].reshape(depth, N_EXPERTS, 1),
        w_gate_up=w_gate_up.astype(BF16), w_down=w_down.astype(BF16),
        ws_gate_up=ws_gate_up.astype(BF16), ws_down=ws_down.astype(BF16),
        ln2_g=ln2_g, ln2_b=ln2_b,
    )
    y_prompt = _trunk(x_prompt, mod_all[:, :Bp], lams, params, depth)
    y_sample = _trunk(x_sample, mod_all[:, Bp:Bp + Bs], lams, params, depth)
    return (y_prompt, y_sample)
```

```python
import functools
import math

import jax
import jax.numpy as jnp
from jax import lax
from jax.experimental import pallas as pl
from jax.experimental.pallas import tpu as pltpu

F32 = jnp.float32
BF16 = jnp.bfloat16

HEAD_DIM = 64
V_DIM = 2 * HEAD_DIM
CONV_TAPS = 31
CONV_HALF = CONV_TAPS // 2
CONV_HALO = 16
N_EXPERTS = 64
N_GROUPS = 8
GROUP_SIZE = N_EXPERTS // N_GROUPS
TOPK_GROUPS = 4
TOP_K = 8
ROUTED_SCALE = 2.5
LN_EPS = 1e-5

V7X_VMEM_BYTES = 64 * 1024 * 1024
VMEM_LIMIT = V7X_VMEM_BYTES - 12 * 1024 * 1024


def _cparams(sem):
    return pltpu.CompilerParams(dimension_semantics=sem, vmem_limit_bytes=VMEM_LIMIT)


def _tile(n, pref, align=8):
    if n <= pref:
        return n
    for t in range(pref - pref % align, 0, -align):
        if n % t == 0:
            return t
    raise ValueError((n, pref, align))


def _ln_rows(v, g, b):
    mu = jnp.mean(v, axis=-1, keepdims=True)
    d = v - mu
    var = jnp.mean(d * d, axis=-1, keepdims=True)
    return d * lax.rsqrt(var + LN_EPS) * g + b


def _silu(v):
    return v * jax.nn.sigmoid(v)


def _mod_kernel(c_ref, w_ref, b_ref, o_ref):
    a = _silu(c_ref[...]).astype(BF16)
    o_ref[...] = jnp.dot(a, w_ref[...].astype(BF16), preferred_element_type=F32) + b_ref[...]


def ada_mod(c, w_ada, b_ada):
    L, D, N = w_ada.shape
    Bt = c.shape[0]
    tn = _tile(N, 1024, 128)
    return pl.pallas_call(
        _mod_kernel,
        grid=(L, N // tn),
        in_specs=[
            pl.BlockSpec((Bt, D), lambda l, j: (0, 0)),
            pl.BlockSpec((None, D, tn), lambda l, j: (l, 0, j)),
            pl.BlockSpec((None, 1, tn), lambda l, j: (l, 0, j)),
        ],
        out_specs=pl.BlockSpec((None, Bt, tn), lambda l, j: (l, 0, j)),
        out_shape=jax.ShapeDtypeStruct((L, Bt, N), F32),
        compiler_params=_cparams(("parallel", "parallel")),
        name="ada_mod",
    )(c, w_ada, b_ada.reshape(L, 1, N))


def _ln_kernel(x_ref, g_ref, b_ref, o_ref):
    o_ref[...] = _ln_rows(x_ref[...], g_ref[...], b_ref[...])


def layer_norm_rows(x, g, b):
    T, D = x.shape
    tm = _tile(T, 512)
    return pl.pallas_call(
        _ln_kernel,
        grid=(T // tm,),
        in_specs=[
            pl.BlockSpec((tm, D), lambda i: (i, 0)),
            pl.BlockSpec((1, D), lambda i: (0, 0)),
            pl.BlockSpec((1, D), lambda i: (0, 0)),
        ],
        out_specs=pl.BlockSpec((tm, D), lambda i: (i, 0)),
        out_shape=jax.ShapeDtypeStruct((T, D), F32),
        compiler_params=_cparams(("parallel",)),
        name="emb_ln",
    )(x, g.reshape(1, D), b.reshape(1, D))


def _inproj_kernel(x_ref, sc_ref, sh_ref, w_ref, b_ref, o_ref, h_scr):
    @pl.when(pl.program_id(1) == 0)
    def _():
        h_scr[...] = (x_ref[...] * (1.0 + sc_ref[...]) + sh_ref[...]).astype(BF16)

    acc = jnp.dot(h_scr[...], w_ref[...], preferred_element_type=F32)
    o_ref[...] = (acc + b_ref[...]).astype(o_ref.dtype)


def in_proj(x, sc, sh, w, layer, b, seq):
    T, D = x.shape
    N = w.shape[2]
    tm = _tile(seq, 512)
    tn = _tile(N, 1024, 128)
    per_b = seq // tm
    return pl.pallas_call(
        _inproj_kernel,
        grid=(T // tm, N // tn),
        in_specs=[
            pl.BlockSpec((tm, D), lambda i, j: (i, 0)),
            pl.BlockSpec((None, 1, D), lambda i, j: (i // per_b, 0, 0)),
            pl.BlockSpec((None, 1, D), lambda i, j: (i // per_b, 0, 0)),
            pl.BlockSpec((None, D, tn), lambda i, j: (layer, 0, j)),
            pl.BlockSpec((1, tn), lambda i, j: (0, j)),
        ],
        out_specs=pl.BlockSpec((tm, tn), lambda i, j: (i, j)),
        out_shape=jax.ShapeDtypeStruct((T, N), BF16),
        scratch_shapes=[pltpu.VMEM((tm, D), BF16)],
        compiler_params=_cparams(("parallel", "arbitrary")),
        name="in_proj",
    )(x, sc, sh, w, b.reshape(1, N))


ATTN_TILE = 256
ATTN_SUBBLOCKS = 2
POS_SPLIT = 256
EXP_CHUNK = 512


def _attn_kernel(lam_ref, q_ref, k_ref, v_ref, g_ref, o_ref, qe_scr, ke_scr, ve_scr, *unit_scr,
                 n_heads, out_scale):
    s_scr = unit_scr[:len(unit_scr) // 2]
    p_scr = unit_scr[len(unit_scr) // 2:]
    h = pl.program_id(1)
    qi = pl.program_id(2)
    S = k_ref.shape[0]
    tq = min(ATTN_TILE, q_ref.shape[0])
    n_sub = q_ref.shape[0] // tq
    nk = S // tq
    lam = lam_ref[0, 0]
    slope = jnp.exp2(-(jnp.full((1, 1), h, jnp.int32) + 1).astype(F32) * (8.0 / n_heads))
    nt_dims = (((1,), (1,)), ((), ()))

    def features(pos, a, b):
        lane = lax.broadcasted_iota(jnp.int32, pos.shape, 1)
        hi = (pos // POS_SPLIT).astype(F32) * (slope * POS_SPLIT)
        lo = (pos % POS_SPLIT).astype(F32) * slope
        f = jnp.where(lane == HEAD_DIM + a[0], hi, 0.0)
        f = jnp.where(lane == HEAD_DIM + a[1], lo, f)
        return jnp.where((lane == HEAD_DIM + b[0]) | (lane == HEAD_DIM + b[1]), 1.0, f)

    @pl.when(qi == 0)
    def _():
        kpos = lax.broadcasted_iota(jnp.int32, (S, V_DIM), 0)
        lane = lax.broadcasted_iota(jnp.int32, (S, V_DIM), 1)
        kfeat = features(kpos, (2, 3), (0, 1))
        kf = k_ref[...].astype(F32)
        ke_scr[0] = jnp.where(lane < HEAD_DIM, kf, kfeat).astype(BF16)
        ke_scr[1] = jnp.where(lane < HEAD_DIM, pltpu.roll(kf, HEAD_DIM, axis=1), kfeat).astype(BF16)
        ve_scr[:, 0:V_DIM] = v_ref[...]
        ve_scr[:, V_DIM:] = jnp.where(lane == 0, 1.0, 0.0).astype(BF16)

    lane = lax.broadcasted_iota(jnp.int32, (tq, V_DIM), 1)
    for sub in range(n_sub):
        qpos = (qi * n_sub + sub) * tq + lax.broadcasted_iota(jnp.int32, (tq, V_DIM), 0)
        qfeat = features(qpos, (0, 1), (2, 3))
        qfeat = jnp.where(lane < HEAD_DIM + 2, -qfeat, qfeat)
        qf = q_ref[sub * tq:(sub + 1) * tq, :].astype(F32) * (HEAD_DIM ** -0.5)
        for j in range(2):
            qj = qf if j == 0 else pltpu.roll(qf, HEAD_DIM, axis=1)
            qe_scr[sub, 0, j] = jnp.where(lane < HEAD_DIM, qj, qfeat).astype(BF16)
            qe_scr[sub, 1, j] = jnp.where(lane < HEAD_DIM, qj, -qfeat).astype(BF16)
            qe_scr[sub, 2, j] = jnp.where(lane < HEAD_DIM, qj, 0.0).astype(BF16)

    rel = (lax.broadcasted_iota(jnp.int32, (tq, tq), 0) - lax.broadcasted_iota(jnp.int32, (tq, tq), 1))
    diag_bias = -slope * jnp.abs(rel).astype(F32)

    units = [(sub, j) for sub in range(n_sub) for j in range(2)]

    def scores(u):
        sub, j = units[u]
        qb = qi * n_sub + sub
        for r in range(nk):
            c = qb + r
            wrapped = c >= nk
            c = jnp.where(wrapped, c - nk, c)
            off = pl.multiple_of(c * tq, tq)
            variant = 2 if r == 0 else jnp.where(wrapped, 0, 1)
            s = lax.dot_general(qe_scr[sub, variant, j], ke_scr[j, pl.ds(off, tq), :], nt_dims,
                                preferred_element_type=F32)
            s_scr[u][:, pl.ds(off, tq)] = s + diag_bias if r == 0 else s

    def row_max(u):
        return jnp.max(s_scr[u][...], axis=1, keepdims=True)

    def exps(u, m):
        w = min(EXP_CHUNK, S)
        for c in range(S // w):
            p_scr[u][:, c * w:(c + 1) * w] = jnp.exp(s_scr[u][:, c * w:(c + 1) * w] - m).astype(BF16)

    def weighted_values(u):
        acc = jnp.dot(p_scr[u][...], ve_scr[...], preferred_element_type=F32)
        return acc[:, 0:V_DIM] / acc[:, V_DIM:V_DIM + 1]

    n_units = len(units)
    maxes, outs = {}, {}
    for t in range(n_units + 3):
        if t < n_units:
            scores(t)
        if 0 <= t - 3 < n_units:
            outs[t - 3] = weighted_values(t - 3)
        if 0 <= t - 2 < n_units:
            exps(t - 2, maxes[t - 2])
        if 0 <= t - 1 < n_units:
            maxes[t - 1] = row_max(t - 1)

    for sub in range(n_sub):
        o = outs[2 * sub] - lam * outs[2 * sub + 1]
        o = o * lax.rsqrt(jnp.mean(o * o, axis=-1, keepdims=True) + LN_EPS) * g_ref[...] * out_scale
        o_ref[sub * tq:(sub + 1) * tq, :] = o.astype(o_ref.dtype)


def diff_attention(P, lam, subln_g, batch, seq, n_heads, out_scale):
    T = P.shape[0]
    tq = _tile(seq, ATTN_SUBBLOCKS * ATTN_TILE)
    sub_rows = min(ATTN_TILE, tq)
    n_units = 2 * (tq // sub_rows)
    nq = seq // tq
    return pl.pallas_call(
        functools.partial(_attn_kernel, n_heads=n_heads, out_scale=out_scale),
        grid=(batch, n_heads, nq),
        in_specs=[
            pl.BlockSpec(memory_space=pltpu.SMEM),
            pl.BlockSpec((tq, V_DIM), lambda b, h, i: (b * nq + i, h)),
            pl.BlockSpec((seq, V_DIM), lambda b, h, i: (b, n_heads + h)),
            pl.BlockSpec((seq, V_DIM), lambda b, h, i: (b, 2 * n_heads + h)),
            pl.BlockSpec((1, V_DIM), lambda b, h, i: (0, 0)),
        ],
        out_specs=pl.BlockSpec((tq, V_DIM), lambda b, h, i: (b * nq + i, h)),
        out_shape=jax.ShapeDtypeStruct((T, n_heads * V_DIM), BF16),
        scratch_shapes=[
            pltpu.VMEM((n_units // 2, 3, 2, sub_rows, V_DIM), BF16),
            pltpu.VMEM((2, seq, V_DIM), BF16),
            pltpu.VMEM((seq, 2 * V_DIM), BF16),
        ] + [pltpu.VMEM((sub_rows, seq), F32)] * n_units + [pltpu.VMEM((sub_rows, seq), BF16)] * n_units,
        compiler_params=_cparams(("parallel", "parallel", "arbitrary")),
        name="diff_attn",
    )(lam.reshape(1, 1), P, P, P, subln_g.reshape(1, V_DIM))


CONV_ROW_CHUNK = 32
CONV_TILE = 256
LANES = 128
SUBLANES = 8


def _conv_kernel(ap_ref, ac_ref, an_ref, gp_ref, gc_ref, gn_ref, w_ref, cb_ref, lg_ref, lb_ref,
                 o_ref, u_scr, y_scr):
    i = pl.program_id(1)
    ts, C = o_ref.shape

    def glu(a_ref, g_ref):
        return a_ref[...].astype(F32) * jax.nn.sigmoid(g_ref[...].astype(F32))

    rows = ts + 2 * CONV_HALO
    u_scr[0, 0:CONV_HALO, :] = jnp.where(i > 0, glu(ap_ref, gp_ref), 0.0)
    u_scr[0, CONV_HALO:CONV_HALO + ts, :] = glu(ac_ref, gc_ref)
    u_scr[0, CONV_HALO + ts:, :] = jnp.where(i < pl.num_programs(1) - 1, glu(an_ref, gn_ref), 0.0)
    for s in range(1, SUBLANES):
        u_scr[s, 0:rows - SUBLANES, :] = u_scr[0, s:s + rows - SUBLANES, :]

    first = CONV_HALO - CONV_HALF
    for c0 in range(0, C, LANES):
        wv = w_ref[:, c0:c0 + LANES]
        cb = cb_ref[:, c0:c0 + LANES]
        for r0 in range(0, ts, CONV_ROW_CHUNK):
            acc = jnp.zeros((CONV_ROW_CHUNK, LANES), F32)
            for t in range(CONV_TAPS):
                s = (first + t) % SUBLANES
                base = r0 + first + t - s
                acc = acc + u_scr[s, base:base + CONV_ROW_CHUNK, c0:c0 + LANES] * wv[t:t + 1, :]
            y_scr[r0:r0 + CONV_ROW_CHUNK, c0:c0 + LANES] = acc + cb

    y = _ln_rows(y_scr[...], lg_ref[...], lb_ref[...])
    o_ref[...] = _silu(y).astype(o_ref.dtype)


def conformer_conv(P, conv_w, conv_b, ln_g, ln_b, batch, seq, col_a, col_g):
    T = P.shape[0]
    C = conv_w.shape[1]
    ts = _tile(seq, CONV_TILE)
    ns = seq // ts
    hb = ts // CONV_HALO
    last_halo = T // CONV_HALO - 1

    def cur(col):
        return pl.BlockSpec((ts, C), lambda b, i: (b * ns + i, col))

    def prev(col):
        return pl.BlockSpec((CONV_HALO, C), lambda b, i: (jnp.maximum((b * ns + i) * hb - 1, 0), col))

    def nxt(col):
        return pl.BlockSpec((CONV_HALO, C), lambda b, i: (jnp.minimum((b * ns + i + 1) * hb, last_halo), col))

    vec = pl.BlockSpec((1, C), lambda b, i: (0, 0))
    return pl.pallas_call(
        _conv_kernel,
        grid=(batch, ns),
        in_specs=[prev(col_a), cur(col_a), nxt(col_a), prev(col_g), cur(col_g), nxt(col_g),
                  pl.BlockSpec((CONV_TAPS, C), lambda b, i: (0, 0)), vec, vec, vec],
        out_specs=pl.BlockSpec((ts, C), lambda b, i: (b * ns + i, 0)),
        out_shape=jax.ShapeDtypeStruct((T, C), BF16),
        scratch_shapes=[pltpu.VMEM((SUBLANES, ts + 2 * CONV_HALO, C), F32), pltpu.VMEM((ts, C), F32)],
        compiler_params=_cparams(("parallel", "arbitrary")),
        name="conformer_conv",
    )(P, P, P, P, P, P, conv_w, conv_b.reshape(1, C), ln_g.reshape(1, C), ln_b.reshape(1, C))


def _outproj_kernel(at_ref, cv_ref, x_ref, w_ref, b_ref, ga_ref, l1g_ref, l1b_ref, sc_ref, sh_ref, wr_ref,
                    x1_ref, h2_ref, lg_ref, *, alpha):
    aw = at_ref.shape[1]
    mix = jnp.dot(at_ref[...], w_ref[0:aw, :], preferred_element_type=F32)
    mix = mix + jnp.dot(cv_ref[...], w_ref[aw:, :], preferred_element_type=F32) + b_ref[...]
    x1 = _ln_rows(alpha * x_ref[...] + ga_ref[...] * mix, l1g_ref[...], l1b_ref[...])
    x1_ref[...] = x1
    h2 = x1 * (1.0 + sc_ref[...]) + sh_ref[...]
    h2_ref[...] = h2
    lg_ref[...] = lax.dot_general(wr_ref[...], h2, (((1,), (1,)), ((), ())),
                                  precision=lax.Precision.HIGHEST, preferred_element_type=F32)


def out_proj(attn, conv, x, w_out, layer, b_out, g_a, ln_g, ln_b, sc_f, sh_f, w_router_t, seq, alpha):
    T, D = x.shape
    aw = attn.shape[1]
    cw = conv.shape[1]
    E = w_router_t.shape[0]
    tm = _tile(seq, 512)
    per_b = seq // tm
    vec = pl.BlockSpec((1, D), lambda i: (0, 0))
    bvec = pl.BlockSpec((None, 1, D), lambda i: (i // per_b, 0, 0))
    row = pl.BlockSpec((tm, D), lambda i: (i, 0))
    once = pl.Buffered(1)
    return pl.pallas_call(
        functools.partial(_outproj_kernel, alpha=alpha),
        grid=(T // tm,),
        in_specs=[
            pl.BlockSpec((tm, aw), lambda i: (i, 0)),
            pl.BlockSpec((tm, cw), lambda i: (i, 0)),
            row,
            pl.BlockSpec((None, aw + cw, D), lambda i: (layer, 0, 0), pipeline_mode=once),
            vec, bvec, vec, vec, bvec, bvec,
            pl.BlockSpec((E, D), lambda i: (0, 0), pipeline_mode=once),
        ],
        out_specs=[row, row, pl.BlockSpec((E, tm), lambda i: (0, i))],
        out_shape=[jax.ShapeDtypeStruct((T, D), F32), jax.ShapeDtypeStruct((T, D), F32),
                   jax.ShapeDtypeStruct((E, T), F32)],
        compiler_params=_cparams(("parallel",)),
        name="out_proj",
    )(attn, conv, x, w_out, b_out.reshape(1, D), g_a, ln_g.reshape(1, D), ln_b.reshape(1, D), sc_f, sh_f,
      w_router_t)


def _route_kernel(lg_ref, rb_ref, eid_ref, gate_ref, pos_ref, cnt_ref, carry):
    step = pl.program_id(0)
    tn = lg_ref.shape[1]

    @pl.when(step == 0)
    def _():
        carry[...] = jnp.zeros_like(carry)

    grp = lax.broadcasted_iota(jnp.int32, (N_GROUPS, tn), 0)
    scores, sel, ids = [], [], []
    for i in range(GROUP_SIZE):
        sc = jax.nn.sigmoid(lg_ref[i * N_GROUPS:(i + 1) * N_GROUPS, :])
        scores.append(sc)
        sel.append(sc + rb_ref[i * N_GROUPS:(i + 1) * N_GROUPS, :])
        ids.append(grp * GROUP_SIZE + i)

    top1 = sel[0]
    top2 = jnp.full_like(top1, -jnp.inf)
    for i in range(1, GROUP_SIZE):
        top2 = jnp.maximum(top2, jnp.minimum(top1, sel[i]))
        top1 = jnp.maximum(top1, sel[i])
    gscore = top1 + top2

    rank = jnp.zeros((N_GROUPS, tn), jnp.int32)
    for r in range(1, N_GROUPS):
        other = pltpu.roll(gscore, r, axis=0)
        other_idx = pltpu.roll(grp, r, axis=0)
        beats = (other > gscore) | ((other == gscore) & (other_idx < grp))
        rank = rank + beats.astype(jnp.int32)
    gmask = rank < TOPK_GROUPS

    masked = [jnp.where(gmask, s, -jnp.inf) for s in sel]
    onehots, gates = [], []
    for _ in range(TOP_K):
        m = masked[0]
        for i in range(1, GROUP_SIZE):
            m = jnp.maximum(m, masked[i])
        m = jnp.max(m, axis=0, keepdims=True)
        cand = jnp.where(masked[0] == m, ids[0], N_EXPERTS)
        for i in range(1, GROUP_SIZE):
            cand = jnp.minimum(cand, jnp.where(masked[i] == m, ids[i], N_EXPERTS))
        chosen = jnp.min(cand, axis=0, keepdims=True)
        oh = [ids[i] == chosen for i in range(GROUP_SIZE)]
        g = jnp.where(oh[0], scores[0], 0.0)
        for i in range(1, GROUP_SIZE):
            g = g + jnp.where(oh[i], scores[i], 0.0)
        gates.append(jnp.sum(g, axis=0, keepdims=True))
        masked = [jnp.where(oh[i], -jnp.inf, masked[i]) for i in range(GROUP_SIZE)]
        onehots.append(oh)
        eid_ref[pl.ds(len(onehots) - 1, 1), :] = chosen

    denom = gates[0]
    for kk in range(1, TOP_K):
        denom = denom + gates[kk]
    for kk in range(TOP_K):
        gate_ref[pl.ds(kk, 1), :] = gates[kk] / denom * ROUTED_SCALE

    chosen_any = []
    for i in range(GROUP_SIZE):
        a = onehots[0][i]
        for kk in range(1, TOP_K):
            a = a | onehots[kk][i]
        chosen_any.append(a.astype(F32))
    mask = jnp.concatenate(chosen_any, axis=0).astype(BF16)
    tri = (lax.broadcasted_iota(jnp.int32, (tn, tn), 0) <= lax.broadcasted_iota(jnp.int32, (tn, tn), 1))
    csum = jnp.dot(mask, tri.astype(BF16), preferred_element_type=F32) + carry[...]
    for kk in range(TOP_K):
        p = jnp.zeros((N_GROUPS, tn), F32)
        for i in range(GROUP_SIZE):
            p = p + jnp.where(onehots[kk][i], csum[i * N_GROUPS:(i + 1) * N_GROUPS, :], 0.0)
        pos_ref[pl.ds(kk, 1), :] = (jnp.sum(p, axis=0, keepdims=True) - 1.0).astype(jnp.int32)
    total = csum[:, tn - 1:tn]
    carry[...] = total
    cnt_ref[...] = jnp.broadcast_to(total, cnt_ref.shape)


def route(logits_t, bias_rows):
    E, T = logits_t.shape
    tn = _tile(T, 512)
    slot = pl.BlockSpec((TOP_K, tn), lambda i: (0, i))
    return pl.pallas_call(
        _route_kernel,
        grid=(T // tn,),
        in_specs=[pl.BlockSpec((E, tn), lambda i: (0, i)), pl.BlockSpec((E, 1), lambda i: (0, 0))],
        out_specs=[slot, slot, slot, pl.BlockSpec((E, LANES), lambda i: (0, 0))],
        out_shape=[jax.ShapeDtypeStruct((TOP_K, T), jnp.int32), jax.ShapeDtypeStruct((TOP_K, T), F32),
                   jax.ShapeDtypeStruct((TOP_K, T), jnp.int32), jax.ShapeDtypeStruct((E, LANES), F32)],
        scratch_shapes=[pltpu.VMEM((E, 1), F32)],
        compiler_params=_cparams(("arbitrary",)),
        name="route",
    )(logits_t, bias_rows)


EXPERT_ROWS = 512
DISPATCH_ROWS = 256


def _dispatch_kernel(tail_ref, dest_ref, h_ref, xs_hbm, zero_buf, sem):
    tm = h_ref.shape[0]
    R = zero_buf.shape[0]

    @pl.when(pl.program_id(0) == 0)
    def _():
        zero_buf[...] = jnp.zeros_like(zero_buf)

        def tail_copy(e):
            start = pl.multiple_of(tail_ref[e], R)
            return pltpu.make_async_copy(zero_buf, xs_hbm.at[pl.ds(start, R), :], sem)

        def start_tail(e, c):
            @pl.when(tail_ref[e] >= 0)
            def _():
                tail_copy(e).start()
            return c

        def wait_tail(e, c):
            @pl.when(tail_ref[e] >= 0)
            def _():
                tail_copy(e).wait()
            return c

        lax.fori_loop(0, 2 * N_EXPERTS, start_tail, 0)
        lax.fori_loop(0, 2 * N_EXPERTS, wait_tail, 0)

    for n in range(TOP_K * tm):
        r = n % tm
        pltpu.make_async_copy(h_ref.at[pl.ds(r, 1), :], xs_hbm.at[pl.ds(dest_ref[0, 0, n], 1), :], sem).start()
    for k in range(TOP_K):
        pltpu.make_async_copy(h_ref, xs_hbm.at[pl.ds(0, tm), :], sem).wait()


def dispatch(dest_tiles, tail, h2, n_rows):
    T, D = h2.shape
    n_t = dest_tiles.shape[0]
    tm = T // n_t
    grid_spec = pltpu.PrefetchScalarGridSpec(
        num_scalar_prefetch=1,
        grid=(n_t,),
        in_specs=[
            pl.BlockSpec((1, 1, TOP_K * tm), lambda i, tail: (i, 0, 0), memory_space=pltpu.SMEM),
            pl.BlockSpec((tm, D), lambda i, tail: (i, 0)),
        ],
        out_specs=pl.BlockSpec(memory_space=pl.ANY),
        scratch_shapes=[pltpu.VMEM((EXPERT_ROWS, D), F32), pltpu.SemaphoreType.DMA],
    )
    return pl.pallas_call(
        _dispatch_kernel,
        grid_spec=grid_spec,
        out_shape=jax.ShapeDtypeStruct((n_rows, D), F32),
        compiler_params=_cparams(("arbitrary",)),
        name="dispatch",
    )(tail, dest_tiles, h2)


def _experts_kernel(blk_e_ref, blk_n_ref, x_ref, wgu_ref, wd_ref, o_ref):
    b = pl.program_id(0)
    F = wd_ref.shape[0]

    @pl.when(blk_n_ref[b] > 0)
    def _():
        gu = jnp.dot(x_ref[...].astype(BF16), wgu_ref[...], preferred_element_type=F32)
        act = (_silu(gu[:, :F]) * gu[:, F:]).astype(BF16)
        o_ref[...] = jnp.dot(act, wd_ref[...], preferred_element_type=F32)

    @pl.when(blk_n_ref[b] == 0)
    def _():
        o_ref[...] = jnp.zeros_like(o_ref)


def routed_experts(xs, blk_e, blk_n, w_gu, w_d, layer):
    n_rows, D = xs.shape
    R = EXPERT_ROWS
    n_blk = n_rows // R
    F = w_d.shape[2]
    grid_spec = pltpu.PrefetchScalarGridSpec(
        num_scalar_prefetch=2,
        grid=(n_blk,),
        in_specs=[
            pl.BlockSpec((R, D), lambda b, be, bn: (jnp.where(bn[b] > 0, b, 0), 0)),
            pl.BlockSpec((None, None, D, 2 * F), lambda b, be, bn: (layer, be[b], 0, 0)),
            pl.BlockSpec((None, None, F, D), lambda b, be, bn: (layer, be[b], 0, 0)),
        ],
        out_specs=pl.BlockSpec((R, D), lambda b, be, bn: (b, 0)),
    )
    return pl.pallas_call(
        _experts_kernel,
        grid_spec=grid_spec,
        out_shape=jax.ShapeDtypeStruct((n_rows, D), F32),
        compiler_params=_cparams(("arbitrary",)),
        name="routed_experts",
    )(blk_e, blk_n, xs, w_gu, w_d)


COMBINE_ROWS = 128


def _combine_kernel(dest_ref, next_dest_ref, y_hbm, gate_ref, h_ref, x_ref, wgu_ref, wd_ref, gf_ref, lg_ref, lb_ref,
                    o_ref, buf, sem, *, alpha):
    i = pl.program_id(0)
    last = pl.num_programs(0) - 1
    tm = h_ref.shape[0] // 2
    F = wd_ref.shape[0]

    def gather(idx_ref, half, slots):
        for k in slots:
            for r in range(tm):
                src = idx_ref[0, 0, (half * TOP_K + k) * tm + r]
                pltpu.make_async_copy(y_hbm.at[pl.ds(src, 1), :], buf.at[half, k, pl.ds(r, 1), :],
                                      sem.at[half]).start()

    def wait_gather(half):
        for k in range(TOP_K):
            pltpu.make_async_copy(y_hbm.at[pl.ds(0, tm), :], buf.at[half, k], sem.at[half]).wait()

    def finish(half, idx_ref, other):
        rows = slice(half * tm, (half + 1) * tm)
        gather(idx_ref, other, (0, 1))
        gu = jnp.dot(h_ref[rows, :].astype(BF16), wgu_ref[...], preferred_element_type=F32)
        gather(idx_ref, other, (2, 3))
        act = (_silu(gu[:, :F]) * gu[:, F:]).astype(BF16)
        ff = jnp.dot(act, wd_ref[...], preferred_element_type=F32)
        gather(idx_ref, other, (4, 5))
        wait_gather(half)
        gate = gate_ref[rows, :]
        for k in range(TOP_K):
            ff = ff + gate[:, k:k + 1] * buf[half, k]
        gather(idx_ref, other, (6, 7))
        o_ref[rows, :] = _ln_rows(alpha * x_ref[rows, :] + gf_ref[...] * ff, lg_ref[...], lb_ref[...])

    @pl.when(i == 0)
    def _():
        gather(dest_ref, 0, range(TOP_K))

    finish(0, dest_ref, 1)
    finish(1, next_dest_ref, 0)

    @pl.when(i == last)
    def _():
        wait_gather(0)


def combine(dest_tiles, y_rows, gate, h2, x1, ws_gu, ws_d, layer, g_f, ln_g, ln_b, seq, alpha):
    T, D = x1.shape
    F = ws_d.shape[1]
    half = T // dest_tiles.shape[0]
    tm = 2 * half
    n_t = T // tm
    dest_steps = dest_tiles.reshape(n_t, 1, 2 * TOP_K * half)
    per_b = seq // tm
    row = pl.BlockSpec((tm, D), lambda i: (i, 0))
    vec = pl.BlockSpec((1, D), lambda i: (0, 0))
    return pl.pallas_call(
        functools.partial(_combine_kernel, alpha=alpha),
        grid=(n_t,),
        in_specs=[
            pl.BlockSpec((1, 1, 2 * TOP_K * half), lambda i: (i, 0, 0), memory_space=pltpu.SMEM),
            pl.BlockSpec((1, 1, 2 * TOP_K * half), lambda i: (jnp.minimum(i + 1, n_t - 1), 0, 0),
                         memory_space=pltpu.SMEM),
            pl.BlockSpec(memory_space=pl.ANY),
            pl.BlockSpec((tm, TOP_K), lambda i: (i, 0)),
            row, row,
            pl.BlockSpec((None, D, 2 * F), lambda i: (layer, 0, 0)),
            pl.BlockSpec((None, F, D), lambda i: (layer, 0, 0)),
            pl.BlockSpec((None, 1, D), lambda i: (i // per_b, 0, 0)),
            vec, vec,
        ],
        out_specs=row,
        out_shape=jax.ShapeDtypeStruct((T, D), F32),
        scratch_shapes=[pltpu.VMEM((2, TOP_K, half, D), F32), pltpu.SemaphoreType.DMA((2,))],
        compiler_params=_cparams(("arbitrary",)),
        name="combine",
    )(dest_steps, dest_steps, y_rows, gate, h2, x1, ws_gu, ws_d, g_f, ln_g.reshape(1, D), ln_b.reshape(1, D))


def _dispatch_tables(eid, pos, counts_rows, T):
    R = EXPERT_ROWS
    n_rows = T * TOP_K + N_EXPERTS * R
    n_blk = n_rows // R
    experts = jnp.arange(N_EXPERTS, dtype=jnp.int32)
    counts = counts_rows.astype(jnp.int32).reshape(GROUP_SIZE, N_GROUPS).T.reshape(N_EXPERTS)
    padded = (counts + R - 1) // R * R
    pad_ends = jnp.cumsum(padded)
    pad_starts = pad_ends - padded
    dest = pos + jnp.sum(jnp.where(eid[..., None] == experts, pad_starts, 0), axis=-1)
    blk_start = jnp.arange(n_blk, dtype=jnp.int32) * R
    blk_e = jnp.minimum(jnp.sum(pad_ends[None, :] <= blk_start[:, None], axis=1), N_EXPERTS - 1).astype(jnp.int32)
    seg_end = jnp.sum(jnp.where(blk_e[:, None] == experts, pad_starts + counts, 0), axis=1)
    blk_n = jnp.clip(seg_end - blk_start, 0, R).astype(jnp.int32)
    unused = pad_ends[-1] + experts * R
    tail = jnp.concatenate([jnp.where(padded > 0, pad_ends - R, -1),
                            jnp.where(unused < n_rows, unused, -1)]).astype(jnp.int32)
    return dest.astype(jnp.int32), blk_e, blk_n, tail, n_rows


def _tile_index(dest, tm):
    K, T = dest.shape
    return dest.reshape(K, T // tm, tm).transpose(1, 0, 2).reshape(T // tm, 1, K * tm)


def _trunk(x, mods, lams, p, depth):
    B, S, D = x.shape
    T = B * S
    aw = D // 2
    n_heads = aw // V_DIM
    alpha = (2 * depth) ** 0.25
    x = layer_norm_rows(x.reshape(T, D), p["emb_ln_g"], p["emb_ln_b"])
    for l in range(depth):
        mod = mods[l]
        sh_a, sc_a, g_a, sh_f, sc_f, g_f = (mod[:, i] for i in range(6))
        P = in_proj(x, sc_a, sh_a, p["w_in"], l, p["b_in"][l], S)
        lam_init = 0.8 - 0.6 * math.exp(-0.3 * l)
        attn = diff_attention(P, lams[l], p["attn_subln_g"][l], B, S, n_heads, 1.0 - lam_init)
        conv = conformer_conv(P, p["conv_w"][l], p["conv_b"][l], p["conv_ln_g"][l], p["conv_ln_b"][l],
                              B, S, 3 * aw // (D - aw), 3 * aw // (D - aw) + 1)
        x1, h2, logits_t = out_proj(attn, conv, x, p["w_out"], l, p["b_out"][l], g_a, p["ln1_g"][l], p["ln1_b"][l],
                                    sc_f, sh_f, p["w_router_t"][l], S, alpha)
        eid, gate, pos, counts_rows = route(logits_t, p["router_bias_rows"][l])
        dest, blk_e, blk_n, tail, n_rows = _dispatch_tables(eid, pos, counts_rows[:, 0], T)
        xs = dispatch(_tile_index(dest, _tile(S, DISPATCH_ROWS)), tail, h2, n_rows)
        y_rows = routed_experts(xs, blk_e, blk_n, p["w_gate_up"], p["w_down"], l)
        x = combine(_tile_index(dest, _tile(S, COMBINE_ROWS)), y_rows, gate.T, h2, x1, p["ws_gate_up"],
                    p["ws_down"], l, g_f, p["ln2_g"][l], p["ln2_b"][l], S, alpha)
    return x.reshape(B, S, D)


def kernel(x_prompt, x_sample, c_prompt, c_sample, emb_ln_g, emb_ln_b, w_ada, b_ada, w_in, b_in, lambda_q1, lambda_k1, lambda_q2, lambda_k2, attn_subln_g, conv_w, conv_b, conv_ln_g, conv_ln_b, w_out, b_out, ln1_g, ln1_b, w_router, router_bias, w_gate_up, w_down, ws_gate_up, ws_down, ln2_g, ln2_b):
    depth, D = w_in.shape[0], w_in.shape[1]
    Bp, Bs = x_prompt.shape[0], x_sample.shape[0]

    c_all = jnp.concatenate([c_prompt, c_sample], axis=0)
    pad = (-c_all.shape[0]) % 8
    c_all = jnp.pad(c_all, ((0, pad), (0, 0)))
    mod_all = ada_mod(c_all, w_ada, b_ada)
    mod_all = mod_all.reshape(depth, c_all.shape[0], 6, 1, D)
    lams = (jnp.exp(jnp.sum(lambda_q1 * lambda_k1, axis=-1)) - jnp.exp(jnp.sum(lambda_q2 * lambda_k2, axis=-1))
            + jnp.asarray([0.8 - 0.6 * math.exp(-0.3 * l) for l in range(depth)], F32))

    rows = jnp.arange(N_EXPERTS)
    perm = (rows % N_GROUPS) * GROUP_SIZE + rows // N_GROUPS
    params = dict(
        emb_ln_g=emb_ln_g, emb_ln_b=emb_ln_b,
        w_in=w_in.astype(BF16), b_in=b_in, attn_subln_g=attn_subln_g,
        conv_w=conv_w, conv_b=conv_b, conv_ln_g=conv_ln_g, conv_ln_b=conv_ln_b,
        w_out=w_out.astype(BF16), b_out=b_out, ln1_g=ln1_g, ln1_b=ln1_b,
        w_router_t=jnp.swapaxes(w_router, 1, 2)[:, perm, :],
        router_bias_rows=router_bias[:, perm].reshape(depth, N_EXPERTS, 1),
        w_gate_up=w_gate_up.astype(BF16), w_down=w_down.astype(BF16),
        ws_gate_up=ws_gate_up.astype(BF16), ws_down=ws_down.astype(BF16),
        ln2_g=ln2_g, ln2_b=ln2_b,
    )
    y_prompt = _trunk(x_prompt, mod_all[:, :Bp], lams, params, depth)
    y_sample = _trunk(x_sample, mod_all[:, Bp:Bp + Bs], lams, params, depth)
    return (y_prompt, y_sample)
```

```python
import functools
import math

import jax
import jax.numpy as jnp
from jax import lax
from jax.experimental import pallas as pl
from jax.experimental.pallas import tpu as pltpu

F32 = jnp.float32
BF16 = jnp.bfloat16

HEAD_DIM = 64
V_DIM = 2 * HEAD_DIM
CONV_TAPS = 31
CONV_HALF = CONV_TAPS // 2
CONV_HALO = 16
N_EXPERTS = 64
N_GROUPS = 8
GROUP_SIZE = N_EXPERTS // N_GROUPS
TOPK_GROUPS = 4
TOP_K = 8
ROUTED_SCALE = 2.5
LN_EPS = 1e-5

V7X_VMEM_BYTES = 64 * 1024 * 1024
VMEM_LIMIT = V7X_VMEM_BYTES - 12 * 1024 * 1024


def _cparams(sem):
    return pltpu.CompilerParams(dimension_semantics=sem, vmem_limit_bytes=VMEM_LIMIT)


def _tile(n, pref, align=8):
    if n <= pref:
        return n
    for t in range(pref - pref % align, 0, -align):
        if n % t == 0:
            return t
    raise ValueError((n, pref, align))


def _ln_rows(v, g, b):
    mu = jnp.mean(v, axis=-1, keepdims=True)
    d = v - mu
    var = jnp.mean(d * d, axis=-1, keepdims=True)
    return d * lax.rsqrt(var + LN_EPS) * g + b


def _silu(v):
    return v * jax.nn.sigmoid(v)


def _mod_kernel(c_ref, w_ref, b_ref, o_ref):
    a = _silu(c_ref[...]).astype(BF16)
    o_ref[...] = jnp.dot(a, w_ref[...].astype(BF16), preferred_element_type=F32) + b_ref[...]


def ada_mod(c, w_ada, b_ada):
    L, D, N = w_ada.shape
    Bt = c.shape[0]
    tn = _tile(N, 1024, 128)
    return pl.pallas_call(
        _mod_kernel,
        grid=(L, N // tn),
        in_specs=[
            pl.BlockSpec((Bt, D), lambda l, j: (0, 0)),
            pl.BlockSpec((None, D, tn), lambda l, j: (l, 0, j)),
            pl.BlockSpec((None, 1, tn), lambda l, j: (l, 0, j)),
        ],
        out_specs=pl.BlockSpec((None, Bt, tn), lambda l, j: (l, 0, j)),
        out_shape=jax.ShapeDtypeStruct((L, Bt, N), F32),
        compiler_params=_cparams(("parallel", "parallel")),
        name="ada_mod",
    )(c, w_ada, b_ada.reshape(L, 1, N))


def _ln_kernel(x_ref, g_ref, b_ref, o_ref):
    o_ref[...] = _ln_rows(x_ref[...], g_ref[...], b_ref[...])


def layer_norm_rows(x, g, b):
    T, D = x.shape
    tm = _tile(T, 512)
    return pl.pallas_call(
        _ln_kernel,
        grid=(T // tm,),
        in_specs=[
            pl.BlockSpec((tm, D), lambda i: (i, 0)),
            pl.BlockSpec((1, D), lambda i: (0, 0)),
            pl.BlockSpec((1, D), lambda i: (0, 0)),
        ],
        out_specs=pl.BlockSpec((tm, D), lambda i: (i, 0)),
        out_shape=jax.ShapeDtypeStruct((T, D), F32),
        compiler_params=_cparams(("parallel",)),
        name="emb_ln",
    )(x, g.reshape(1, D), b.reshape(1, D))


def _inproj_kernel(x_ref, sc_ref, sh_ref, w_ref, b_ref, o_ref, h_scr):
    @pl.when(pl.program_id(1) == 0)
    def _():
        h_scr[...] = (x_ref[...] * (1.0 + sc_ref[...]) + sh_ref[...]).astype(BF16)

    acc = jnp.dot(h_scr[...], w_ref[...], preferred_element_type=F32)
    o_ref[...] = (acc + b_ref[...]).astype(o_ref.dtype)


def in_proj(x, sc, sh, w, layer, b, seq):
    T, D = x.shape
    N = w.shape[2]
    tm = _tile(seq, 1024)
    tn = _tile(N, 1024, 128)
    per_b = seq // tm
    return pl.pallas_call(
        _inproj_kernel,
        grid=(T // tm, N // tn),
        in_specs=[
            pl.BlockSpec((tm, D), lambda i, j: (i, 0)),
            pl.BlockSpec((None, 1, D), lambda i, j: (i // per_b, 0, 0)),
            pl.BlockSpec((None, 1, D), lambda i, j: (i // per_b, 0, 0)),
            pl.BlockSpec((None, D, tn), lambda i, j: (layer, 0, j)),
            pl.BlockSpec((1, tn), lambda i, j: (0, j)),
        ],
        out_specs=pl.BlockSpec((tm, tn), lambda i, j: (i, j)),
        out_shape=jax.ShapeDtypeStruct((T, N), BF16),
        scratch_shapes=[pltpu.VMEM((tm, D), BF16)],
        compiler_params=_cparams(("parallel", "arbitrary")),
        name="in_proj",
    )(x, sc, sh, w, b.reshape(1, N))


ATTN_TILE = 256
ATTN_SUBBLOCKS = 2
POS_SPLIT = 256
EXP_CHUNK = 512


def _attn_kernel(lam_ref, q_ref, k_ref, v_ref, g_ref, o_ref, qe_scr, ke_scr, ve_scr, *unit_scr,
                 n_heads, out_scale):
    s_scr = unit_scr[:len(unit_scr) // 2]
    p_scr = unit_scr[len(unit_scr) // 2:]
    h = pl.program_id(1)
    qi = pl.program_id(2)
    S = k_ref.shape[0]
    tq = min(ATTN_TILE, q_ref.shape[0])
    n_sub = q_ref.shape[0] // tq
    nk = S // tq
    lam = lam_ref[0, 0]
    slope = jnp.exp2(-(jnp.full((1, 1), h, jnp.int32) + 1).astype(F32) * (8.0 / n_heads))
    nt_dims = (((1,), (1,)), ((), ()))

    def features(pos, a, b):
        lane = lax.broadcasted_iota(jnp.int32, pos.shape, 1)
        hi = (pos // POS_SPLIT).astype(F32) * (slope * POS_SPLIT)
        lo = (pos % POS_SPLIT).astype(F32) * slope
        f = jnp.where(lane == HEAD_DIM + a[0], hi, 0.0)
        f = jnp.where(lane == HEAD_DIM + a[1], lo, f)
        return jnp.where((lane == HEAD_DIM + b[0]) | (lane == HEAD_DIM + b[1]), 1.0, f)

    @pl.when(qi == 0)
    def _():
        kpos = lax.broadcasted_iota(jnp.int32, (S, V_DIM), 0)
        lane = lax.broadcasted_iota(jnp.int32, (S, V_DIM), 1)
        kfeat = features(kpos, (2, 3), (0, 1))
        kf = k_ref[...].astype(F32)
        ke_scr[0] = jnp.where(lane < HEAD_DIM, kf, kfeat).astype(BF16)
        ke_scr[1] = jnp.where(lane < HEAD_DIM, pltpu.roll(kf, HEAD_DIM, axis=1), kfeat).astype(BF16)
        ve_scr[:, 0:V_DIM] = v_ref[...]
        ve_scr[:, V_DIM:] = jnp.where(lane == 0, 1.0, 0.0).astype(BF16)

    lane = lax.broadcasted_iota(jnp.int32, (tq, V_DIM), 1)
    for sub in range(n_sub):
        qpos = (qi * n_sub + sub) * tq + lax.broadcasted_iota(jnp.int32, (tq, V_DIM), 0)
        qfeat = features(qpos, (0, 1), (2, 3))
        qfeat = jnp.where(lane < HEAD_DIM + 2, -qfeat, qfeat)
        qf = q_ref[sub * tq:(sub + 1) * tq, :].astype(F32) * (HEAD_DIM ** -0.5)
        for j in range(2):
            qj = qf if j == 0 else pltpu.roll(qf, HEAD_DIM, axis=1)
            qe_scr[sub, 0, j] = jnp.where(lane < HEAD_DIM, qj, qfeat).astype(BF16)
            qe_scr[sub, 1, j] = jnp.where(lane < HEAD_DIM, qj, -qfeat).astype(BF16)
            qe_scr[sub, 2, j] = jnp.where(lane < HEAD_DIM, qj, 0.0).astype(BF16)

    rel = (lax.broadcasted_iota(jnp.int32, (tq, tq), 0) - lax.broadcasted_iota(jnp.int32, (tq, tq), 1))
    diag_bias = -slope * jnp.abs(rel).astype(F32)

    units = [(sub, j) for sub in range(n_sub) for j in range(2)]

    def scores(u):
        sub, j = units[u]
        qb = qi * n_sub + sub
        running = None
        for r in range(nk):
            c = qb + r
            wrapped = c >= nk
            c = jnp.where(wrapped, c - nk, c)
            off = pl.multiple_of(c * tq, tq)
            variant = 2 if r == 0 else jnp.where(wrapped, 0, 1)
            s = lax.dot_general(qe_scr[sub, variant, j], ke_scr[j, pl.ds(off, tq), :], nt_dims,
                                preferred_element_type=F32)
            if r == 0:
                s = s + diag_bias
            s_scr[u][:, pl.ds(off, tq)] = s
            for l0 in range(0, tq, LANES):
                part = s[:, l0:l0 + LANES]
                running = part if running is None else jnp.maximum(running, part)
        return jnp.max(running, axis=1, keepdims=True)

    def exps(u, m):
        w = min(EXP_CHUNK, S)
        for c in range(S // w):
            p_scr[u][:, c * w:(c + 1) * w] = jnp.exp(s_scr[u][:, c * w:(c + 1) * w] - m).astype(BF16)

    def weighted_values(u):
        acc = jnp.dot(p_scr[u][...], ve_scr[...], preferred_element_type=F32)
        return acc[:, 0:V_DIM] / acc[:, V_DIM:V_DIM + 1]

    n_units = len(units)
    maxes, outs = {}, {}
    for t in range(n_units + 2):
        if t < n_units:
            maxes[t] = scores(t)
        if 0 <= t - 2 < n_units:
            outs[t - 2] = weighted_values(t - 2)
        if 0 <= t - 1 < n_units:
            exps(t - 1, maxes[t - 1])

    for sub in range(n_sub):
        o = outs[2 * sub] - lam * outs[2 * sub + 1]
        o = o * lax.rsqrt(jnp.mean(o * o, axis=-1, keepdims=True) + LN_EPS) * g_ref[...] * out_scale
        o_ref[sub * tq:(sub + 1) * tq, :] = o.astype(o_ref.dtype)


def diff_attention(P, lam, subln_g, batch, seq, n_heads, out_scale):
    T = P.shape[0]
    tq = _tile(seq, ATTN_SUBBLOCKS * ATTN_TILE)
    sub_rows = min(ATTN_TILE, tq)
    n_units = 2 * (tq // sub_rows)
    nq = seq // tq
    return pl.pallas_call(
        functools.partial(_attn_kernel, n_heads=n_heads, out_scale=out_scale),
        grid=(batch, n_heads, nq),
        in_specs=[
            pl.BlockSpec(memory_space=pltpu.SMEM),
            pl.BlockSpec((tq, V_DIM), lambda b, h, i: (b * nq + i, h)),
            pl.BlockSpec((seq, V_DIM), lambda b, h, i: (b, n_heads + h)),
            pl.BlockSpec((seq, V_DIM), lambda b, h, i: (b, 2 * n_heads + h)),
            pl.BlockSpec((1, V_DIM), lambda b, h, i: (0, 0)),
        ],
        out_specs=pl.BlockSpec((tq, V_DIM), lambda b, h, i: (b * nq + i, h)),
        out_shape=jax.ShapeDtypeStruct((T, n_heads * V_DIM), BF16),
        scratch_shapes=[
            pltpu.VMEM((n_units // 2, 3, 2, sub_rows, V_DIM), BF16),
            pltpu.VMEM((2, seq, V_DIM), BF16),
            pltpu.VMEM((seq, 2 * V_DIM), BF16),
        ] + [pltpu.VMEM((sub_rows, seq), F32)] * n_units + [pltpu.VMEM((sub_rows, seq), BF16)] * n_units,
        compiler_params=_cparams(("parallel", "parallel", "arbitrary")),
        name="diff_attn",
    )(lam.reshape(1, 1), P, P, P, subln_g.reshape(1, V_DIM))


CONV_ROW_CHUNK = 32
CONV_TILE = 256
LANES = 128
SUBLANES = 8


def _conv_kernel(ap_ref, ac_ref, an_ref, gp_ref, gc_ref, gn_ref, w_ref, cb_ref, lg_ref, lb_ref,
                 o_ref, u_scr, y_scr):
    i = pl.program_id(1)
    ts, C = o_ref.shape

    def glu(a_ref, g_ref):
        return a_ref[...].astype(F32) * jax.nn.sigmoid(g_ref[...].astype(F32))

    rows = ts + 2 * CONV_HALO
    u_scr[0, 0:CONV_HALO, :] = jnp.where(i > 0, glu(ap_ref, gp_ref), 0.0)
    u_scr[0, CONV_HALO:CONV_HALO + ts, :] = glu(ac_ref, gc_ref)
    u_scr[0, CONV_HALO + ts:, :] = jnp.where(i < pl.num_programs(1) - 1, glu(an_ref, gn_ref), 0.0)
    for s in range(1, SUBLANES):
        u_scr[s, 0:rows - SUBLANES, :] = u_scr[0, s:s + rows - SUBLANES, :]

    first = CONV_HALO - CONV_HALF
    for c0 in range(0, C, LANES):
        wv = w_ref[:, c0:c0 + LANES]
        cb = cb_ref[:, c0:c0 + LANES]
        for r0 in range(0, ts, CONV_ROW_CHUNK):
            acc = jnp.zeros((CONV_ROW_CHUNK, LANES), F32)
            for t in range(CONV_TAPS):
                s = (first + t) % SUBLANES
                base = r0 + first + t - s
                acc = acc + u_scr[s, base:base + CONV_ROW_CHUNK, c0:c0 + LANES] * wv[t:t + 1, :]
            y_scr[r0:r0 + CONV_ROW_CHUNK, c0:c0 + LANES] = acc + cb

    y = _ln_rows(y_scr[...], lg_ref[...], lb_ref[...])
    o_ref[...] = _silu(y).astype(o_ref.dtype)


def conformer_conv(P, conv_w, conv_b, ln_g, ln_b, batch, seq, col_a, col_g):
    T = P.shape[0]
    C = conv_w.shape[1]
    ts = _tile(seq, CONV_TILE)
    ns = seq // ts
    hb = ts // CONV_HALO
    last_halo = T // CONV_HALO - 1

    def cur(col):
        return pl.BlockSpec((ts, C), lambda b, i: (b * ns + i, col))

    def prev(col):
        return pl.BlockSpec((CONV_HALO, C), lambda b, i: (jnp.maximum((b * ns + i) * hb - 1, 0), col))

    def nxt(col):
        return pl.BlockSpec((CONV_HALO, C), lambda b, i: (jnp.minimum((b * ns + i + 1) * hb, last_halo), col))

    vec = pl.BlockSpec((1, C), lambda b, i: (0, 0))
    return pl.pallas_call(
        _conv_kernel,
        grid=(batch, ns),
        in_specs=[prev(col_a), cur(col_a), nxt(col_a), prev(col_g), cur(col_g), nxt(col_g),
                  pl.BlockSpec((CONV_TAPS, C), lambda b, i: (0, 0)), vec, vec, vec],
        out_specs=pl.BlockSpec((ts, C), lambda b, i: (b * ns + i, 0)),
        out_shape=jax.ShapeDtypeStruct((T, C), BF16),
        scratch_shapes=[pltpu.VMEM((SUBLANES, ts + 2 * CONV_HALO, C), F32), pltpu.VMEM((ts, C), F32)],
        compiler_params=_cparams(("parallel", "arbitrary")),
        name="conformer_conv",
    )(P, P, P, P, P, P, conv_w, conv_b.reshape(1, C), ln_g.reshape(1, C), ln_b.reshape(1, C))


def _outproj_kernel(at_ref, cv_ref, x_ref, w_ref, b_ref, ga_ref, l1g_ref, l1b_ref, sc_ref, sh_ref, wr_ref,
                    x1_ref, h2_ref, lg_ref, *, alpha):
    aw = at_ref.shape[1]
    mix = jnp.dot(at_ref[...], w_ref[0:aw, :], preferred_element_type=F32)
    mix = mix + jnp.dot(cv_ref[...], w_ref[aw:, :], preferred_element_type=F32) + b_ref[...]
    x1 = _ln_rows(alpha * x_ref[...] + ga_ref[...] * mix, l1g_ref[...], l1b_ref[...])
    x1_ref[...] = x1
    h2 = x1 * (1.0 + sc_ref[...]) + sh_ref[...]
    h2_ref[...] = h2
    lg_ref[...] = lax.dot_general(wr_ref[...], h2, (((1,), (1,)), ((), ())),
                                  precision=lax.Precision.HIGHEST, preferred_element_type=F32)


def out_proj(attn, conv, x, w_out, layer, b_out, g_a, ln_g, ln_b, sc_f, sh_f, w_router_t, seq, alpha):
    T, D = x.shape
    aw = attn.shape[1]
    cw = conv.shape[1]
    E = w_router_t.shape[0]
    tm = _tile(seq, 512)
    per_b = seq // tm
    vec = pl.BlockSpec((1, D), lambda i: (0, 0))
    bvec = pl.BlockSpec((None, 1, D), lambda i: (i // per_b, 0, 0))
    row = pl.BlockSpec((tm, D), lambda i: (i, 0))
    once = pl.Buffered(1)
    return pl.pallas_call(
        functools.partial(_outproj_kernel, alpha=alpha),
        grid=(T // tm,),
        in_specs=[
            pl.BlockSpec((tm, aw), lambda i: (i, 0)),
            pl.BlockSpec((tm, cw), lambda i: (i, 0)),
            row,
            pl.BlockSpec((None, aw + cw, D), lambda i: (layer, 0, 0), pipeline_mode=once),
            vec, bvec, vec, vec, bvec, bvec,
            pl.BlockSpec((E, D), lambda i: (0, 0), pipeline_mode=once),
        ],
        out_specs=[row, row, pl.BlockSpec((E, tm), lambda i: (0, i))],
        out_shape=[jax.ShapeDtypeStruct((T, D), F32), jax.ShapeDtypeStruct((T, D), F32),
                   jax.ShapeDtypeStruct((E, T), F32)],
        compiler_params=_cparams(("parallel",)),
        name="out_proj",
    )(attn, conv, x, w_out, b_out.reshape(1, D), g_a, ln_g.reshape(1, D), ln_b.reshape(1, D), sc_f, sh_f,
      w_router_t)


def _route_kernel(lg_ref, rb_ref, eid_ref, gate_ref, pos_ref, cnt_ref, carry):
    step = pl.program_id(0)
    tn = lg_ref.shape[1]

    @pl.when(step == 0)
    def _():
        carry[...] = jnp.zeros_like(carry)

    grp = lax.broadcasted_iota(jnp.int32, (N_GROUPS, tn), 0)
    scores, sel, ids = [], [], []
    for i in range(GROUP_SIZE):
        sc = jax.nn.sigmoid(lg_ref[i * N_GROUPS:(i + 1) * N_GROUPS, :])
        scores.append(sc)
        sel.append(sc + rb_ref[i * N_GROUPS:(i + 1) * N_GROUPS, :])
        ids.append(grp * GROUP_SIZE + i)

    top1 = sel[0]
    top2 = jnp.full_like(top1, -jnp.inf)
    for i in range(1, GROUP_SIZE):
        top2 = jnp.maximum(top2, jnp.minimum(top1, sel[i]))
        top1 = jnp.maximum(top1, sel[i])
    gscore = top1 + top2

    rank = jnp.zeros((N_GROUPS, tn), jnp.int32)
    for r in range(1, N_GROUPS):
        other = pltpu.roll(gscore, r, axis=0)
        other_idx = pltpu.roll(grp, r, axis=0)
        beats = (other > gscore) | ((other == gscore) & (other_idx < grp))
        rank = rank + beats.astype(jnp.int32)
    gmask = rank < TOPK_GROUPS

    masked = [jnp.where(gmask, s, -jnp.inf) for s in sel]
    onehots, gates = [], []
    for _ in range(TOP_K):
        m = masked[0]
        for i in range(1, GROUP_SIZE):
            m = jnp.maximum(m, masked[i])
        m = jnp.max(m, axis=0, keepdims=True)
        cand = jnp.where(masked[0] == m, ids[0], N_EXPERTS)
        for i in range(1, GROUP_SIZE):
            cand = jnp.minimum(cand, jnp.where(masked[i] == m, ids[i], N_EXPERTS))
        chosen = jnp.min(cand, axis=0, keepdims=True)
        oh = [ids[i] == chosen for i in range(GROUP_SIZE)]
        g = jnp.where(oh[0], scores[0], 0.0)
        for i in range(1, GROUP_SIZE):
            g = g + jnp.where(oh[i], scores[i], 0.0)
        gates.append(jnp.sum(g, axis=0, keepdims=True))
        masked = [jnp.where(oh[i], -jnp.inf, masked[i]) for i in range(GROUP_SIZE)]
        onehots.append(oh)
        eid_ref[pl.ds(len(onehots) - 1, 1), :] = chosen

    denom = gates[0]
    for kk in range(1, TOP_K):
        denom = denom + gates[kk]
    for kk in range(TOP_K):
        gate_ref[pl.ds(kk, 1), :] = gates[kk] / denom * ROUTED_SCALE

    chosen_any = []
    for i in range(GROUP_SIZE):
        a = onehots[0][i]
        for kk in range(1, TOP_K):
            a = a | onehots[kk][i]
        chosen_any.append(a.astype(F32))
    mask = jnp.concatenate(chosen_any, axis=0).astype(BF16)
    tri = (lax.broadcasted_iota(jnp.int32, (tn, tn), 0) <= lax.broadcasted_iota(jnp.int32, (tn, tn), 1))
    csum = jnp.dot(mask, tri.astype(BF16), preferred_element_type=F32) + carry[...]
    for kk in range(TOP_K):
        p = jnp.zeros((N_GROUPS, tn), F32)
        for i in range(GROUP_SIZE):
            p = p + jnp.where(onehots[kk][i], csum[i * N_GROUPS:(i + 1) * N_GROUPS, :], 0.0)
        pos_ref[pl.ds(kk, 1), :] = (jnp.sum(p, axis=0, keepdims=True) - 1.0).astype(jnp.int32)
    total = csum[:, tn - 1:tn]
    carry[...] = total
    cnt_ref[...] = jnp.broadcast_to(total, cnt_ref.shape)


def route(logits_t, bias_rows):
    E, T = logits_t.shape
    tn = _tile(T, 512)
    slot = pl.BlockSpec((TOP_K, tn), lambda i: (0, i))
    return pl.pallas_call(
        _route_kernel,
        grid=(T // tn,),
        in_specs=[pl.BlockSpec((E, tn), lambda i: (0, i)), pl.BlockSpec((E, 1), lambda i: (0, 0))],
        out_specs=[slot, slot, slot, pl.BlockSpec((E, LANES), lambda i: (0, 0))],
        out_shape=[jax.ShapeDtypeStruct((TOP_K, T), jnp.int32), jax.ShapeDtypeStruct((TOP_K, T), F32),
                   jax.ShapeDtypeStruct((TOP_K, T), jnp.int32), jax.ShapeDtypeStruct((E, LANES), F32)],
        scratch_shapes=[pltpu.VMEM((E, 1), F32)],
        compiler_params=_cparams(("arbitrary",)),
        name="route",
    )(logits_t, bias_rows)


EXPERT_ROWS = 512
DISPATCH_ROWS = 256


def _dispatch_kernel(tail_ref, dest_ref, h_ref, xs_hbm, zero_buf, sem):
    tm = h_ref.shape[0]
    R = zero_buf.shape[0]

    @pl.when(pl.program_id(0) == 0)
    def _():
        zero_buf[...] = jnp.zeros_like(zero_buf)

        def tail_copy(e):
            start = pl.multiple_of(tail_ref[e], R)
            return pltpu.make_async_copy(zero_buf, xs_hbm.at[pl.ds(start, R), :], sem)

        def start_tail(e, c):
            @pl.when(tail_ref[e] >= 0)
            def _():
                tail_copy(e).start()
            return c

        def wait_tail(e, c):
            @pl.when(tail_ref[e] >= 0)
            def _():
                tail_copy(e).wait()
            return c

        lax.fori_loop(0, 2 * N_EXPERTS, start_tail, 0)
        lax.fori_loop(0, 2 * N_EXPERTS, wait_tail, 0)

    for n in range(TOP_K * tm):
        r = n % tm
        pltpu.make_async_copy(h_ref.at[pl.ds(r, 1), :], xs_hbm.at[pl.ds(dest_ref[0, 0, n], 1), :], sem).start()
    for k in range(TOP_K):
        pltpu.make_async_copy(h_ref, xs_hbm.at[pl.ds(0, tm), :], sem).wait()


def dispatch(dest_tiles, tail, h2, n_rows):
    T, D = h2.shape
    n_t = dest_tiles.shape[0]
    tm = T // n_t
    grid_spec = pltpu.PrefetchScalarGridSpec(
        num_scalar_prefetch=1,
        grid=(n_t,),
        in_specs=[
            pl.BlockSpec((1, 1, TOP_K * tm), lambda i, tail: (i, 0, 0), memory_space=pltpu.SMEM),
            pl.BlockSpec((tm, D), lambda i, tail: (i, 0)),
        ],
        out_specs=pl.BlockSpec(memory_space=pl.ANY),
        scratch_shapes=[pltpu.VMEM((EXPERT_ROWS, D), F32), pltpu.SemaphoreType.DMA],
    )
    return pl.pallas_call(
        _dispatch_kernel,
        grid_spec=grid_spec,
        out_shape=jax.ShapeDtypeStruct((n_rows, D), F32),
        compiler_params=_cparams(("arbitrary",)),
        name="dispatch",
    )(tail, dest_tiles, h2)


def _experts_kernel(blk_e_ref, blk_n_ref, x_ref, wgu_ref, wd_ref, o_ref):
    b = pl.program_id(0)
    F = wd_ref.shape[0]

    @pl.when(blk_n_ref[b] > 0)
    def _():
        gu = jnp.dot(x_ref[...].astype(BF16), wgu_ref[...], preferred_element_type=F32)
        act = (_silu(gu[:, :F]) * gu[:, F:]).astype(BF16)
        o_ref[...] = jnp.dot(act, wd_ref[...], preferred_element_type=F32)

    @pl.when(blk_n_ref[b] == 0)
    def _():
        o_ref[...] = jnp.zeros_like(o_ref)


def routed_experts(xs, blk_e, blk_n, w_gu, w_d, layer):
    n_rows, D = xs.shape
    R = EXPERT_ROWS
    n_blk = n_rows // R
    F = w_d.shape[2]
    grid_spec = pltpu.PrefetchScalarGridSpec(
        num_scalar_prefetch=2,
        grid=(n_blk,),
        in_specs=[
            pl.BlockSpec((R, D), lambda b, be, bn: (jnp.where(bn[b] > 0, b, 0), 0)),
            pl.BlockSpec((None, None, D, 2 * F), lambda b, be, bn: (layer, be[b], 0, 0)),
            pl.BlockSpec((None, None, F, D), lambda b, be, bn: (layer, be[b], 0, 0)),
        ],
        out_specs=pl.BlockSpec((R, D), lambda b, be, bn: (b, 0)),
    )
    return pl.pallas_call(
        _experts_kernel,
        grid_spec=grid_spec,
        out_shape=jax.ShapeDtypeStruct((n_rows, D), F32),
        compiler_params=_cparams(("arbitrary",)),
        name="routed_experts",
    )(blk_e, blk_n, xs, w_gu, w_d)


COMBINE_ROWS = 128


def _combine_kernel(dest_ref, next_dest_ref, y_hbm, gate_ref, h_ref, x_ref, wgu_ref, wd_ref, gf_ref, lg_ref, lb_ref,
                    o_ref, buf, sem, *, alpha):
    i = pl.program_id(0)
    last = pl.num_programs(0) - 1
    tm = h_ref.shape[0] // 2
    F = wd_ref.shape[0]

    def gather(idx_ref, half, slots):
        for k in slots:
            for r in range(tm):
                src = idx_ref[0, 0, (half * TOP_K + k) * tm + r]
                pltpu.make_async_copy(y_hbm.at[pl.ds(src, 1), :], buf.at[half, k, pl.ds(r, 1), :],
                                      sem.at[half]).start()

    def wait_gather(half):
        for k in range(TOP_K):
            pltpu.make_async_copy(y_hbm.at[pl.ds(0, tm), :], buf.at[half, k], sem.at[half]).wait()

    def finish(half, idx_ref, other):
        rows = slice(half * tm, (half + 1) * tm)
        gather(idx_ref, other, (0, 1))
        gu = jnp.dot(h_ref[rows, :].astype(BF16), wgu_ref[...], preferred_element_type=F32)
        gather(idx_ref, other, (2, 3))
        act = (_silu(gu[:, :F]) * gu[:, F:]).astype(BF16)
        ff = jnp.dot(act, wd_ref[...], preferred_element_type=F32)
        gather(idx_ref, other, (4, 5))
        wait_gather(half)
        gate = gate_ref[rows, :]
        for k in range(TOP_K):
            ff = ff + gate[:, k:k + 1] * buf[half, k]
        gather(idx_ref, other, (6, 7))
        o_ref[rows, :] = _ln_rows(alpha * x_ref[rows, :] + gf_ref[...] * ff, lg_ref[...], lb_ref[...])

    @pl.when(i == 0)
    def _():
        gather(dest_ref, 0, range(TOP_K))

    finish(0, dest_ref, 1)
    finish(1, next_dest_ref, 0)

    @pl.when(i == last)
    def _():
        wait_gather(0)


def combine(dest_tiles, y_rows, gate, h2, x1, ws_gu, ws_d, layer, g_f, ln_g, ln_b, seq, alpha):
    T, D = x1.shape
    F = ws_d.shape[1]
    half = T // dest_tiles.shape[0]
    tm = 2 * half
    n_t = T // tm
    dest_steps = dest_tiles.reshape(n_t, 1, 2 * TOP_K * half)
    per_b = seq // tm
    row = pl.BlockSpec((tm, D), lambda i: (i, 0))
    vec = pl.BlockSpec((1, D), lambda i: (0, 0))
    return pl.pallas_call(
        functools.partial(_combine_kernel, alpha=alpha),
        grid=(n_t,),
        in_specs=[
            pl.BlockSpec((1, 1, 2 * TOP_K * half), lambda i: (i, 0, 0), memory_space=pltpu.SMEM),
            pl.BlockSpec((1, 1, 2 * TOP_K * half), lambda i: (jnp.minimum(i + 1, n_t - 1), 0, 0),
                         memory_space=pltpu.SMEM),
            pl.BlockSpec(memory_space=pl.ANY),
            pl.BlockSpec((tm, TOP_K), lambda i: (i, 0)),
            row, row,
            pl.BlockSpec((None, D, 2 * F), lambda i: (layer, 0, 0)),
            pl.BlockSpec((None, F, D), lambda i: (layer, 0, 0)),
            pl.BlockSpec((None, 1, D), lambda i: (i // per_b, 0, 0)),
            vec, vec,
        ],
        out_specs=row,
        out_shape=jax.ShapeDtypeStruct((T, D), F32),
        scratch_shapes=[pltpu.VMEM((2, TOP_K, half, D), F32), pltpu.SemaphoreType.DMA((2,))],
        compiler_params=_cparams(("arbitrary",)),
        name="combine",
    )(dest_steps, dest_steps, y_rows, gate, h2, x1, ws_gu, ws_d, g_f, ln_g.reshape(1, D), ln_b.reshape(1, D))


def _dispatch_tables(eid, pos, counts_rows, T):
    R = EXPERT_ROWS
    n_rows = T * TOP_K + N_EXPERTS * R
    n_blk = n_rows // R
    experts = jnp.arange(N_EXPERTS, dtype=jnp.int32)
    counts = counts_rows.astype(jnp.int32).reshape(GROUP_SIZE, N_GROUPS).T.reshape(N_EXPERTS)
    padded = (counts + R - 1) // R * R
    pad_ends = jnp.cumsum(padded)
    pad_starts = pad_ends - padded
    dest = pos + jnp.sum(jnp.where(eid[..., None] == experts, pad_starts, 0), axis=-1)
    blk_start = jnp.arange(n_blk, dtype=jnp.int32) * R
    blk_e = jnp.minimum(jnp.sum(pad_ends[None, :] <= blk_start[:, None], axis=1), N_EXPERTS - 1).astype(jnp.int32)
    seg_end = jnp.sum(jnp.where(blk_e[:, None] == experts, pad_starts + counts, 0), axis=1)
    blk_n = jnp.clip(seg_end - blk_start, 0, R).astype(jnp.int32)
    unused = pad_ends[-1] + experts * R
    tail = jnp.concatenate([jnp.where(padded > 0, pad_ends - R, -1),
                            jnp.where(unused < n_rows, unused, -1)]).astype(jnp.int32)
    return dest.astype(jnp.int32), blk_e, blk_n, tail, n_rows


def _tile_index(dest, tm):
    K, T = dest.shape
    return dest.reshape(K, T // tm, tm).transpose(1, 0, 2).reshape(T // tm, 1, K * tm)


def _trunk(x, mods, lams, p, depth):
    B, S, D = x.shape
    T = B * S
    aw = D // 2
    n_heads = aw // V_DIM
    alpha = (2 * depth) ** 0.25
    x = layer_norm_rows(x.reshape(T, D), p["emb_ln_g"], p["emb_ln_b"])
    for l in range(depth):
        mod = mods[l]
        sh_a, sc_a, g_a, sh_f, sc_f, g_f = (mod[:, i] for i in range(6))
        P = in_proj(x, sc_a, sh_a, p["w_in"], l, p["b_in"][l], S)
        lam_init = 0.8 - 0.6 * math.exp(-0.3 * l)
        attn = diff_attention(P, lams[l], p["attn_subln_g"][l], B, S, n_heads, 1.0 - lam_init)
        conv = conformer_conv(P, p["conv_w"][l], p["conv_b"][l], p["conv_ln_g"][l], p["conv_ln_b"][l],
                              B, S, 3 * aw // (D - aw), 3 * aw // (D - aw) + 1)
        x1, h2, logits_t = out_proj(attn, conv, x, p["w_out"], l, p["b_out"][l], g_a, p["ln1_g"][l], p["ln1_b"][l],
                                    sc_f, sh_f, p["w_router_t"][l], S, alpha)
        eid, gate, pos, counts_rows = route(logits_t, p["router_bias_rows"][l])
        dest, blk_e, blk_n, tail, n_rows = _dispatch_tables(eid, pos, counts_rows[:, 0], T)
        xs = dispatch(_tile_index(dest, _tile(S, DISPATCH_ROWS)), tail, h2, n_rows)
        y_rows = routed_experts(xs, blk_e, blk_n, p["w_gate_up"], p["w_down"], l)
        x = combine(_tile_index(dest, _tile(S, COMBINE_ROWS)), y_rows, gate.T, h2, x1, p["ws_gate_up"],
                    p["ws_down"], l, g_f, p["ln2_g"][l], p["ln2_b"][l], S, alpha)
    return x.reshape(B, S, D)


def kernel(x_prompt, x_sample, c_prompt, c_sample, emb_ln_g, emb_ln_b, w_ada, b_ada, w_in, b_in, lambda_q1, lambda_k1, lambda_q2, lambda_k2, attn_subln_g, conv_w, conv_b, conv_ln_g, conv_ln_b, w_out, b_out, ln1_g, ln1_b, w_router, router_bias, w_gate_up, w_down, ws_gate_up, ws_down, ln2_g, ln2_b):
    depth, D = w_in.shape[0], w_in.shape[1]
    Bp, Bs = x_prompt.shape[0], x_sample.shape[0]

    c_all = jnp.concatenate([c_prompt, c_sample], axis=0)
    pad = (-c_all.shape[0]) % 8
    c_all = jnp.pad(c_all, ((0, pad), (0, 0)))
    mod_all = ada_mod(c_all, w_ada, b_ada)
    mod_all = mod_all.reshape(depth, c_all.shape[0], 6, 1, D)
    lams = (jnp.exp(jnp.sum(lambda_q1 * lambda_k1, axis=-1)) - jnp.exp(jnp.sum(lambda_q2 * lambda_k2, axis=-1))
            + jnp.asarray([0.8 - 0.6 * math.exp(-0.3 * l) for l in range(depth)], F32))

    rows = jnp.arange(N_EXPERTS)
    perm = (rows % N_GROUPS) * GROUP_SIZE + rows // N_GROUPS
    params = dict(
        emb_ln_g=emb_ln_g, emb_ln_b=emb_ln_b,
        w_in=w_in.astype(BF16), b_in=b_in, attn_subln_g=attn_subln_g,
        conv_w=conv_w, conv_b=conv_b, conv_ln_g=conv_ln_g, conv_ln_b=conv_ln_b,
        w_out=w_out.astype(BF16), b_out=b_out, ln1_g=ln1_g, ln1_b=ln1_b,
        w_router_t=jnp.swapaxes(w_router, 1, 2)[:, perm, :],
        router_bias_rows=router_bias[:, perm].reshape(depth, N_EXPERTS, 1),
        w_gate_up=w_gate_up.astype(BF16), w_down=w_down.astype(BF16),
        ws_gate_up=ws_gate_up.astype(BF16), ws_down=ws_down.astype(BF16),
        ln2_g=ln2_g, ln2_b=ln2_b,
    )
    y_prompt = _trunk(x_prompt, mod_all[:, :Bp], lams, params, depth)
    y_sample = _trunk(x_sample, mod_all[:, Bp:Bp + Bs], lams, params, depth)
    return (y_prompt, y_sample)
```

```python
import functools
import math

import jax
import jax.numpy as jnp
from jax import lax
from jax.experimental import pallas as pl
from jax.experimental.pallas import tpu as pltpu

F32 = jnp.float32
BF16 = jnp.bfloat16

HEAD_DIM = 64
V_DIM = 2 * HEAD_DIM
CONV_TAPS = 31
CONV_HALF = CONV_TAPS // 2
CONV_HALO = 16
N_EXPERTS = 64
N_GROUPS = 8
GROUP_SIZE = N_EXPERTS // N_GROUPS
TOPK_GROUPS = 4
TOP_K = 8
ROUTED_SCALE = 2.5
LN_EPS = 1e-5

V7X_VMEM_BYTES = 64 * 1024 * 1024
VMEM_LIMIT = V7X_VMEM_BYTES - 12 * 1024 * 1024


def _cparams(sem):
    return pltpu.CompilerParams(dimension_semantics=sem, vmem_limit_bytes=VMEM_LIMIT)


def _tile(n, pref, align=8):
    if n <= pref:
        return n
    for t in range(pref - pref % align, 0, -align):
        if n % t == 0:
            return t
    raise ValueError((n, pref, align))


def _ln_rows(v, g, b):
    mu = jnp.mean(v, axis=-1, keepdims=True)
    d = v - mu
    var = jnp.mean(d * d, axis=-1, keepdims=True)
    return d * lax.rsqrt(var + LN_EPS) * g + b


def _silu(v):
    return v * jax.nn.sigmoid(v)


def _mod_kernel(c_ref, w_ref, b_ref, o_ref):
    a = _silu(c_ref[...]).astype(BF16)
    o_ref[...] = jnp.dot(a, w_ref[...].astype(BF16), preferred_element_type=F32) + b_ref[...]


def ada_mod(c, w_ada, b_ada):
    L, D, N = w_ada.shape
    Bt = c.shape[0]
    tn = _tile(N, 1024, 128)
    return pl.pallas_call(
        _mod_kernel,
        grid=(L, N // tn),
        in_specs=[
            pl.BlockSpec((Bt, D), lambda l, j: (0, 0)),
            pl.BlockSpec((None, D, tn), lambda l, j: (l, 0, j)),
            pl.BlockSpec((None, 1, tn), lambda l, j: (l, 0, j)),
        ],
        out_specs=pl.BlockSpec((None, Bt, tn), lambda l, j: (l, 0, j)),
        out_shape=jax.ShapeDtypeStruct((L, Bt, N), F32),
        compiler_params=_cparams(("parallel", "parallel")),
        name="ada_mod",
    )(c, w_ada, b_ada.reshape(L, 1, N))


def _ln_kernel(x_ref, g_ref, b_ref, o_ref):
    o_ref[...] = _ln_rows(x_ref[...], g_ref[...], b_ref[...])


def layer_norm_rows(x, g, b):
    T, D = x.shape
    tm = _tile(T, 512)
    return pl.pallas_call(
        _ln_kernel,
        grid=(T // tm,),
        in_specs=[
            pl.BlockSpec((tm, D), lambda i: (i, 0)),
            pl.BlockSpec((1, D), lambda i: (0, 0)),
            pl.BlockSpec((1, D), lambda i: (0, 0)),
        ],
        out_specs=pl.BlockSpec((tm, D), lambda i: (i, 0)),
        out_shape=jax.ShapeDtypeStruct((T, D), F32),
        compiler_params=_cparams(("parallel",)),
        name="emb_ln",
    )(x, g.reshape(1, D), b.reshape(1, D))


def _inproj_kernel(x_ref, sc_ref, sh_ref, w_ref, b_ref, o_ref, h_scr):
    @pl.when(pl.program_id(1) == 0)
    def _():
        h_scr[...] = (x_ref[...] * (1.0 + sc_ref[...]) + sh_ref[...]).astype(BF16)

    acc = jnp.dot(h_scr[...], w_ref[...], preferred_element_type=F32)
    o_ref[...] = (acc + b_ref[...]).astype(o_ref.dtype)


def in_proj(x, sc, sh, w, layer, b, seq):
    T, D = x.shape
    N = w.shape[2]
    tm = _tile(seq, 1024)
    tn = _tile(N, 1024, 128)
    per_b = seq // tm
    return pl.pallas_call(
        _inproj_kernel,
        grid=(T // tm, N // tn),
        in_specs=[
            pl.BlockSpec((tm, D), lambda i, j: (i, 0)),
            pl.BlockSpec((None, 1, D), lambda i, j: (i // per_b, 0, 0)),
            pl.BlockSpec((None, 1, D), lambda i, j: (i // per_b, 0, 0)),
            pl.BlockSpec((None, D, tn), lambda i, j: (layer, 0, j)),
            pl.BlockSpec((1, tn), lambda i, j: (0, j)),
        ],
        out_specs=pl.BlockSpec((tm, tn), lambda i, j: (i, j)),
        out_shape=jax.ShapeDtypeStruct((T, N), BF16),
        scratch_shapes=[pltpu.VMEM((tm, D), BF16)],
        compiler_params=_cparams(("parallel", "arbitrary")),
        name="in_proj",
    )(x, sc, sh, w, b.reshape(1, N))


ATTN_TILE = 256
ATTN_SUBBLOCKS = 2
POS_SPLIT = 256
EXP_CHUNK = 512


def _attn_kernel(lam_ref, q_ref, k_ref, v_ref, g_ref, o_ref, qe_scr, ke_scr, ve_scr, *unit_scr,
                 n_heads, out_scale):
    s_scr = unit_scr[:len(unit_scr) // 2]
    p_scr = unit_scr[len(unit_scr) // 2:]
    h = pl.program_id(1)
    qi = pl.program_id(2)
    S = k_ref.shape[0]
    tq = min(ATTN_TILE, q_ref.shape[0])
    n_sub = q_ref.shape[0] // tq
    nk = S // tq
    lam = lam_ref[0, 0]
    slope = jnp.exp2(-(jnp.full((1, 1), h, jnp.int32) + 1).astype(F32) * (8.0 / n_heads))
    nt_dims = (((1,), (1,)), ((), ()))

    def features(pos, a, b):
        lane = lax.broadcasted_iota(jnp.int32, pos.shape, 1)
        hi = (pos // POS_SPLIT).astype(F32) * (slope * POS_SPLIT)
        lo = (pos % POS_SPLIT).astype(F32) * slope
        f = jnp.where(lane == HEAD_DIM + a[0], hi, 0.0)
        f = jnp.where(lane == HEAD_DIM + a[1], lo, f)
        return jnp.where((lane == HEAD_DIM + b[0]) | (lane == HEAD_DIM + b[1]), 1.0, f)

    @pl.when(qi == 0)
    def _():
        kpos = lax.broadcasted_iota(jnp.int32, (S, V_DIM), 0)
        lane = lax.broadcasted_iota(jnp.int32, (S, V_DIM), 1)
        kfeat = features(kpos, (2, 3), (0, 1))
        kf = k_ref[...].astype(F32)
        ke_scr[0] = jnp.where(lane < HEAD_DIM, kf, kfeat).astype(BF16)
        ke_scr[1] = jnp.where(lane < HEAD_DIM, pltpu.roll(kf, HEAD_DIM, axis=1), kfeat).astype(BF16)
        ve_scr[:, 0:V_DIM] = v_ref[...]
        ve_scr[:, V_DIM:] = jnp.where(lane == 0, 1.0, 0.0).astype(BF16)

    lane = lax.broadcasted_iota(jnp.int32, (tq, V_DIM), 1)
    for sub in range(n_sub):
        qpos = (qi * n_sub + sub) * tq + lax.broadcasted_iota(jnp.int32, (tq, V_DIM), 0)
        qfeat = features(qpos, (0, 1), (2, 3))
        qfeat = jnp.where(lane < HEAD_DIM + 2, -qfeat, qfeat)
        qf = q_ref[sub * tq:(sub + 1) * tq, :].astype(F32) * (HEAD_DIM ** -0.5)
        for j in range(2):
            qj = qf if j == 0 else pltpu.roll(qf, HEAD_DIM, axis=1)
            qe_scr[sub, 0, j] = jnp.where(lane < HEAD_DIM, qj, qfeat).astype(BF16)
            qe_scr[sub, 1, j] = jnp.where(lane < HEAD_DIM, qj, -qfeat).astype(BF16)
            qe_scr[sub, 2, j] = jnp.where(lane < HEAD_DIM, qj, 0.0).astype(BF16)

    rel = (lax.broadcasted_iota(jnp.int32, (tq, tq), 0) - lax.broadcasted_iota(jnp.int32, (tq, tq), 1))
    diag_bias = -slope * jnp.abs(rel).astype(F32)

    units = [(sub, j) for sub in range(n_sub) for j in range(2)]

    def scores(u):
        sub, j = units[u]
        qb = qi * n_sub + sub
        for r in range(nk):
            c = qb + r
            wrapped = c >= nk
            c = jnp.where(wrapped, c - nk, c)
            off = pl.multiple_of(c * tq, tq)
            variant = 2 if r == 0 else jnp.where(wrapped, 0, 1)
            s = lax.dot_general(qe_scr[sub, variant, j], ke_scr[j, pl.ds(off, tq), :], nt_dims,
                                preferred_element_type=F32)
            s_scr[u][:, pl.ds(off, tq)] = s + diag_bias if r == 0 else s

    def row_max(u):
        return jnp.max(s_scr[u][...], axis=1, keepdims=True)

    def exps(u, m):
        w = min(EXP_CHUNK, S)
        for c in range(S // w):
            p_scr[u][:, c * w:(c + 1) * w] = jnp.exp(s_scr[u][:, c * w:(c + 1) * w] - m).astype(BF16)

    def weighted_values(u):
        acc = jnp.dot(p_scr[u][...], ve_scr[...], preferred_element_type=F32)
        return acc[:, 0:V_DIM] / acc[:, V_DIM:V_DIM + 1]

    n_units = len(units)
    maxes, outs = {}, {}
    for t in range(n_units + 3):
        if t < n_units:
            scores(t)
        if 0 <= t - 3 < n_units:
            outs[t - 3] = weighted_values(t - 3)
        if 0 <= t - 2 < n_units:
            exps(t - 2, maxes[t - 2])
        if 0 <= t - 1 < n_units:
            maxes[t - 1] = row_max(t - 1)

    for sub in range(n_sub):
        o = outs[2 * sub] - lam * outs[2 * sub + 1]
        o = o * lax.rsqrt(jnp.mean(o * o, axis=-1, keepdims=True) + LN_EPS) * g_ref[...] * out_scale
        o_ref[sub * tq:(sub + 1) * tq, :] = o.astype(o_ref.dtype)


def diff_attention(P, lam, subln_g, batch, seq, n_heads, out_scale):
    T = P.shape[0]
    tq = _tile(seq, ATTN_SUBBLOCKS * ATTN_TILE)
    sub_rows = min(ATTN_TILE, tq)
    n_units = 2 * (tq // sub_rows)
    nq = seq // tq
    return pl.pallas_call(
        functools.partial(_attn_kernel, n_heads=n_heads, out_scale=out_scale),
        grid=(batch, n_heads, nq),
        in_specs=[
            pl.BlockSpec(memory_space=pltpu.SMEM),
            pl.BlockSpec((tq, V_DIM), lambda b, h, i: (b * nq + i, h)),
            pl.BlockSpec((seq, V_DIM), lambda b, h, i: (b, n_heads + h)),
            pl.BlockSpec((seq, V_DIM), lambda b, h, i: (b, 2 * n_heads + h)),
            pl.BlockSpec((1, V_DIM), lambda b, h, i: (0, 0)),
        ],
        out_specs=pl.BlockSpec((tq, V_DIM), lambda b, h, i: (b * nq + i, h)),
        out_shape=jax.ShapeDtypeStruct((T, n_heads * V_DIM), BF16),
        scratch_shapes=[
            pltpu.VMEM((n_units // 2, 3, 2, sub_rows, V_DIM), BF16),
            pltpu.VMEM((2, seq, V_DIM), BF16),
            pltpu.VMEM((seq, 2 * V_DIM), BF16),
        ] + [pltpu.VMEM((sub_rows, seq), F32)] * n_units + [pltpu.VMEM((sub_rows, seq), BF16)] * n_units,
        compiler_params=_cparams(("parallel", "parallel", "arbitrary")),
        name="diff_attn",
    )(lam.reshape(1, 1), P, P, P, subln_g.reshape(1, V_DIM))


CONV_ROW_CHUNK = 32
CONV_TILE = 256
LANES = 128
SUBLANES = 8


def _conv_kernel(ap_ref, ac_ref, an_ref, gp_ref, gc_ref, gn_ref, w_ref, cb_ref, lg_ref, lb_ref,
                 o_ref, u_scr, y_scr):
    i = pl.program_id(1)
    ts, C = o_ref.shape

    def glu(a_ref, g_ref):
        return a_ref[...].astype(F32) * jax.nn.sigmoid(g_ref[...].astype(F32))

    rows = ts + 2 * CONV_HALO
    u_scr[0, 0:CONV_HALO, :] = jnp.where(i > 0, glu(ap_ref, gp_ref), 0.0)
    u_scr[0, CONV_HALO:CONV_HALO + ts, :] = glu(ac_ref, gc_ref)
    u_scr[0, CONV_HALO + ts:, :] = jnp.where(i < pl.num_programs(1) - 1, glu(an_ref, gn_ref), 0.0)
    for s in range(1, SUBLANES):
        u_scr[s, 0:rows - SUBLANES, :] = u_scr[0, s:s + rows - SUBLANES, :]

    first = CONV_HALO - CONV_HALF
    for c0 in range(0, C, LANES):
        wv = w_ref[:, c0:c0 + LANES]
        cb = cb_ref[:, c0:c0 + LANES]
        for r0 in range(0, ts, CONV_ROW_CHUNK):
            acc = jnp.zeros((CONV_ROW_CHUNK, LANES), F32)
            for t in range(CONV_TAPS):
                s = (first + t) % SUBLANES
                base = r0 + first + t - s
                acc = acc + u_scr[s, base:base + CONV_ROW_CHUNK, c0:c0 + LANES] * wv[t:t + 1, :]
            y_scr[r0:r0 + CONV_ROW_CHUNK, c0:c0 + LANES] = acc + cb

    y = _ln_rows(y_scr[...], lg_ref[...], lb_ref[...])
    o_ref[...] = _silu(y).astype(o_ref.dtype)


def conformer_conv(P, conv_w, conv_b, ln_g, ln_b, batch, seq, col_a, col_g):
    T = P.shape[0]
    C = conv_w.shape[1]
    ts = _tile(seq, CONV_TILE)
    ns = seq // ts
    hb = ts // CONV_HALO
    last_halo = T // CONV_HALO - 1

    def cur(col):
        return pl.BlockSpec((ts, C), lambda b, i: (b * ns + i, col))

    def prev(col):
        return pl.BlockSpec((CONV_HALO, C), lambda b, i: (jnp.maximum((b * ns + i) * hb - 1, 0), col))

    def nxt(col):
        return pl.BlockSpec((CONV_HALO, C), lambda b, i: (jnp.minimum((b * ns + i + 1) * hb, last_halo), col))

    vec = pl.BlockSpec((1, C), lambda b, i: (0, 0))
    return pl.pallas_call(
        _conv_kernel,
        grid=(batch, ns),
        in_specs=[prev(col_a), cur(col_a), nxt(col_a), prev(col_g), cur(col_g), nxt(col_g),
                  pl.BlockSpec((CONV_TAPS, C), lambda b, i: (0, 0)), vec, vec, vec],
        out_specs=pl.BlockSpec((ts, C), lambda b, i: (b * ns + i, 0)),
        out_shape=jax.ShapeDtypeStruct((T, C), BF16),
        scratch_shapes=[pltpu.VMEM((SUBLANES, ts + 2 * CONV_HALO, C), F32), pltpu.VMEM((ts, C), F32)],
        compiler_params=_cparams(("parallel", "arbitrary")),
        name="conformer_conv",
    )(P, P, P, P, P, P, conv_w, conv_b.reshape(1, C), ln_g.reshape(1, C), ln_b.reshape(1, C))


def _outproj_kernel(at_ref, cv_ref, x_ref, w_ref, b_ref, ga_ref, l1g_ref, l1b_ref, sc_ref, sh_ref, wr_ref,
                    x1_ref, h2_ref, lg_ref, *, alpha):
    aw = at_ref.shape[1]
    mix = jnp.dot(at_ref[...], w_ref[0:aw, :], preferred_element_type=F32)
    mix = mix + jnp.dot(cv_ref[...], w_ref[aw:, :], preferred_element_type=F32) + b_ref[...]
    x1 = _ln_rows(alpha * x_ref[...] + ga_ref[...] * mix, l1g_ref[...], l1b_ref[...])
    x1_ref[...] = x1
    h2 = x1 * (1.0 + sc_ref[...]) + sh_ref[...]
    h2_ref[...] = h2
    lg_ref[...] = lax.dot_general(wr_ref[...], h2, (((1,), (1,)), ((), ())),
                                  precision=lax.Precision.HIGHEST, preferred_element_type=F32)


def out_proj(attn, conv, x, w_out, layer, b_out, g_a, ln_g, ln_b, sc_f, sh_f, w_router_t, seq, alpha):
    T, D = x.shape
    aw = attn.shape[1]
    cw = conv.shape[1]
    E = w_router_t.shape[0]
    tm = _tile(seq, 512)
    per_b = seq // tm
    vec = pl.BlockSpec((1, D), lambda i: (0, 0))
    bvec = pl.BlockSpec((None, 1, D), lambda i: (i // per_b, 0, 0))
    row = pl.BlockSpec((tm, D), lambda i: (i, 0))
    once = pl.Buffered(1)
    return pl.pallas_call(
        functools.partial(_outproj_kernel, alpha=alpha),
        grid=(T // tm,),
        in_specs=[
            pl.BlockSpec((tm, aw), lambda i: (i, 0)),
            pl.BlockSpec((tm, cw), lambda i: (i, 0)),
            row,
            pl.BlockSpec((None, aw + cw, D), lambda i: (layer, 0, 0), pipeline_mode=once),
            vec, bvec, vec, vec, bvec, bvec,
            pl.BlockSpec((E, D), lambda i: (0, 0), pipeline_mode=once),
        ],
        out_specs=[row, row, pl.BlockSpec((E, tm), lambda i: (0, i))],
        out_shape=[jax.ShapeDtypeStruct((T, D), F32), jax.ShapeDtypeStruct((T, D), F32),
                   jax.ShapeDtypeStruct((E, T), F32)],
        compiler_params=_cparams(("parallel",)),
        name="out_proj",
    )(attn, conv, x, w_out, b_out.reshape(1, D), g_a, ln_g.reshape(1, D), ln_b.reshape(1, D), sc_f, sh_f,
      w_router_t)


def _route_kernel(lg_ref, rb_ref, eid_ref, gate_ref, pos_ref, cnt_ref, carry):
    step = pl.program_id(0)
    tn = lg_ref.shape[1]

    @pl.when(step == 0)
    def _():
        carry[...] = jnp.zeros_like(carry)

    grp = lax.broadcasted_iota(jnp.int32, (N_GROUPS, tn), 0)
    scores, sel, ids = [], [], []
    for i in range(GROUP_SIZE):
        sc = jax.nn.sigmoid(lg_ref[i * N_GROUPS:(i + 1) * N_GROUPS, :])
        scores.append(sc)
        sel.append(sc + rb_ref[i * N_GROUPS:(i + 1) * N_GROUPS, :])
        ids.append(grp * GROUP_SIZE + i)

    top1 = sel[0]
    top2 = jnp.full_like(top1, -jnp.inf)
    for i in range(1, GROUP_SIZE):
        top2 = jnp.maximum(top2, jnp.minimum(top1, sel[i]))
        top1 = jnp.maximum(top1, sel[i])
    gscore = top1 + top2

    rank = jnp.zeros((N_GROUPS, tn), jnp.int32)
    for r in range(1, N_GROUPS):
        other = pltpu.roll(gscore, r, axis=0)
        other_idx = pltpu.roll(grp, r, axis=0)
        beats = (other > gscore) | ((other == gscore) & (other_idx < grp))
        rank = rank + beats.astype(jnp.int32)
    gmask = rank < TOPK_GROUPS

    masked = [jnp.where(gmask, s, -jnp.inf) for s in sel]
    onehots, gates = [], []
    for _ in range(TOP_K):
        m = masked[0]
        for i in range(1, GROUP_SIZE):
            m = jnp.maximum(m, masked[i])
        m = jnp.max(m, axis=0, keepdims=True)
        cand = jnp.where(masked[0] == m, ids[0], N_EXPERTS)
        for i in range(1, GROUP_SIZE):
            cand = jnp.minimum(cand, jnp.where(masked[i] == m, ids[i], N_EXPERTS))
        chosen = jnp.min(cand, axis=0, keepdims=True)
        oh = [ids[i] == chosen for i in range(GROUP_SIZE)]
        g = jnp.where(oh[0], scores[0], 0.0)
        for i in range(1, GROUP_SIZE):
            g = g + jnp.where(oh[i], scores[i], 0.0)
        gates.append(jnp.sum(g, axis=0, keepdims=True))
        masked = [jnp.where(oh[i], -jnp.inf, masked[i]) for i in range(GROUP_SIZE)]
        onehots.append(oh)
        eid_ref[pl.ds(len(onehots) - 1, 1), :] = chosen

    denom = gates[0]
    for kk in range(1, TOP_K):
        denom = denom + gates[kk]
    for kk in range(TOP_K):
        gate_ref[pl.ds(kk, 1), :] = gates[kk] / denom * ROUTED_SCALE

    chosen_any = []
    for i in range(GROUP_SIZE):
        a = onehots[0][i]
        for kk in range(1, TOP_K):
            a = a | onehots[kk][i]
        chosen_any.append(a.astype(F32))
    mask = jnp.concatenate(chosen_any, axis=0).astype(BF16)
    tri = (lax.broadcasted_iota(jnp.int32, (tn, tn), 0) <= lax.broadcasted_iota(jnp.int32, (tn, tn), 1))
    csum = jnp.dot(mask, tri.astype(BF16), preferred_element_type=F32) + carry[...]
    for kk in range(TOP_K):
        p = jnp.zeros((N_GROUPS, tn), F32)
        for i in range(GROUP_SIZE):
            p = p + jnp.where(onehots[kk][i], csum[i * N_GROUPS:(i + 1) * N_GROUPS, :], 0.0)
        pos_ref[pl.ds(kk, 1), :] = (jnp.sum(p, axis=0, keepdims=True) - 1.0).astype(jnp.int32)
    total = csum[:, tn - 1:tn]
    carry[...] = total
    cnt_ref[...] = jnp.broadcast_to(total, cnt_ref.shape)


def route(logits_t, bias_rows):
    E, T = logits_t.shape
    tn = _tile(T, 512)
    slot = pl.BlockSpec((TOP_K, tn), lambda i: (0, i))
    return pl.pallas_call(
        _route_kernel,
        grid=(T // tn,),
        in_specs=[pl.BlockSpec((E, tn), lambda i: (0, i)), pl.BlockSpec((E, 1), lambda i: (0, 0))],
        out_specs=[slot, slot, slot, pl.BlockSpec((E, LANES), lambda i: (0, 0))],
        out_shape=[jax.ShapeDtypeStruct((TOP_K, T), jnp.int32), jax.ShapeDtypeStruct((TOP_K, T), F32),
                   jax.ShapeDtypeStruct((TOP_K, T), jnp.int32), jax.ShapeDtypeStruct((E, LANES), F32)],
        scratch_shapes=[pltpu.VMEM((E, 1), F32)],
        compiler_params=_cparams(("arbitrary",)),
        name="route",
    )(logits_t, bias_rows)


EXPERT_ROWS = 512
DISPATCH_ROWS = 256


def _dispatch_kernel(tail_ref, dest_ref, h_ref, xs_hbm, zero_buf, sem):
    tm = h_ref.shape[0]
    R = zero_buf.shape[0]

    @pl.when(pl.program_id(0) == 0)
    def _():
        zero_buf[...] = jnp.zeros_like(zero_buf)

        def tail_copy(e):
            start = pl.multiple_of(tail_ref[e], R)
            return pltpu.make_async_copy(zero_buf, xs_hbm.at[pl.ds(start, R), :], sem)

        def start_tail(e, c):
            @pl.when(tail_ref[e] >= 0)
            def _():
                tail_copy(e).start()
            return c

        def wait_tail(e, c):
            @pl.when(tail_ref[e] >= 0)
            def _():
                tail_copy(e).wait()
            return c

        lax.fori_loop(0, 2 * N_EXPERTS, start_tail, 0)
        lax.fori_loop(0, 2 * N_EXPERTS, wait_tail, 0)

    for n in range(TOP_K * tm):
        r = n % tm
        pltpu.make_async_copy(h_ref.at[pl.ds(r, 1), :], xs_hbm.at[pl.ds(dest_ref[0, 0, n], 1), :], sem).start()
    for k in range(TOP_K):
        pltpu.make_async_copy(h_ref, xs_hbm.at[pl.ds(0, tm), :], sem).wait()


def dispatch(dest_tiles, tail, h2, n_rows):
    T, D = h2.shape
    n_t = dest_tiles.shape[0]
    tm = T // n_t
    grid_spec = pltpu.PrefetchScalarGridSpec(
        num_scalar_prefetch=1,
        grid=(n_t,),
        in_specs=[
            pl.BlockSpec((1, 1, TOP_K * tm), lambda i, tail: (i, 0, 0), memory_space=pltpu.SMEM),
            pl.BlockSpec((tm, D), lambda i, tail: (i, 0)),
        ],
        out_specs=pl.BlockSpec(memory_space=pl.ANY),
        scratch_shapes=[pltpu.VMEM((EXPERT_ROWS, D), F32), pltpu.SemaphoreType.DMA],
    )
    return pl.pallas_call(
        _dispatch_kernel,
        grid_spec=grid_spec,
        out_shape=jax.ShapeDtypeStruct((n_rows, D), F32),
        compiler_params=_cparams(("arbitrary",)),
        name="dispatch",
    )(tail, dest_tiles, h2)


def _experts_kernel(blk_e_ref, blk_n_ref, x_ref, wgu_ref, wd_ref, o_ref):
    b = pl.program_id(0)
    F = wd_ref.shape[0]

    @pl.when(blk_n_ref[b] > 0)
    def _():
        gu = jnp.dot(x_ref[...].astype(BF16), wgu_ref[...], preferred_element_type=F32)
        act = (_silu(gu[:, :F]) * gu[:, F:]).astype(BF16)
        o_ref[...] = jnp.dot(act, wd_ref[...], preferred_element_type=F32)

    @pl.when(blk_n_ref[b] == 0)
    def _():
        o_ref[...] = jnp.zeros_like(o_ref)


def routed_experts(xs, blk_e, blk_n, w_gu, w_d, layer):
    n_rows, D = xs.shape
    R = EXPERT_ROWS
    n_blk = n_rows // R
    F = w_d.shape[2]
    grid_spec = pltpu.PrefetchScalarGridSpec(
        num_scalar_prefetch=2,
        grid=(n_blk,),
        in_specs=[
            pl.BlockSpec((R, D), lambda b, be, bn: (jnp.where(bn[b] > 0, b, 0), 0)),
            pl.BlockSpec((None, None, D, 2 * F), lambda b, be, bn: (layer, be[b], 0, 0)),
            pl.BlockSpec((None, None, F, D), lambda b, be, bn: (layer, be[b], 0, 0)),
        ],
        out_specs=pl.BlockSpec((R, D), lambda b, be, bn: (b, 0)),
    )
    return pl.pallas_call(
        _experts_kernel,
        grid_spec=grid_spec,
        out_shape=jax.ShapeDtypeStruct((n_rows, D), F32),
        compiler_params=_cparams(("arbitrary",)),
        name="routed_experts",
    )(blk_e, blk_n, xs, w_gu, w_d)


COMBINE_ROWS = 128


def _combine_kernel(dest_ref, next_dest_ref, y_hbm, gate_ref, h_ref, x_ref, wgu_ref, wd_ref, gf_ref, lg_ref, lb_ref,
                    o_ref, buf, sem, *, alpha):
    i = pl.program_id(0)
    last = pl.num_programs(0) - 1
    tm = h_ref.shape[0] // 2
    F = wd_ref.shape[0]

    def gather(idx_ref, half, slots):
        for k in slots:
            for r in range(tm):
                src = idx_ref[0, 0, (half * TOP_K + k) * tm + r]
                pltpu.make_async_copy(y_hbm.at[pl.ds(src, 1), :], buf.at[half, k, pl.ds(r, 1), :],
                                      sem.at[half]).start()

    def wait_gather(half):
        for k in range(TOP_K):
            pltpu.make_async_copy(y_hbm.at[pl.ds(0, tm), :], buf.at[half, k], sem.at[half]).wait()

    def finish(half, idx_ref, other):
        rows = slice(half * tm, (half + 1) * tm)
        gather(idx_ref, other, (0, 1))
        gu = jnp.dot(h_ref[rows, :].astype(BF16), wgu_ref[...], preferred_element_type=F32)
        gather(idx_ref, other, (2, 3))
        act = (_silu(gu[:, :F]) * gu[:, F:]).astype(BF16)
        ff = jnp.dot(act, wd_ref[...], preferred_element_type=F32)
        gather(idx_ref, other, (4, 5))
        wait_gather(half)
        gate = gate_ref[rows, :]
        for k in range(TOP_K):
            ff = ff + gate[:, k:k + 1] * buf[half, k]
        gather(idx_ref, other, (6, 7))
        o_ref[rows, :] = _ln_rows(alpha * x_ref[rows, :] + gf_ref[...] * ff, lg_ref[...], lb_ref[...])

    @pl.when(i == 0)
    def _():
        gather(dest_ref, 0, range(TOP_K))

    finish(0, dest_ref, 1)
    finish(1, next_dest_ref, 0)

    @pl.when(i == last)
    def _():
        wait_gather(0)


def combine(dest_tiles, y_rows, gate, h2, x1, ws_gu, ws_d, layer, g_f, ln_g, ln_b, seq, alpha):
    T, D = x1.shape
    F = ws_d.shape[1]
    half = T // dest_tiles.shape[0]
    tm = 2 * half
    n_t = T // tm
    dest_steps = dest_tiles.reshape(n_t, 1, 2 * TOP_K * half)
    per_b = seq // tm
    row = pl.BlockSpec((tm, D), lambda i: (i, 0))
    vec = pl.BlockSpec((1, D), lambda i: (0, 0))
    return pl.pallas_call(
        functools.partial(_combine_kernel, alpha=alpha),
        grid=(n_t,),
        in_specs=[
            pl.BlockSpec((1, 1, 2 * TOP_K * half), lambda i: (i, 0, 0), memory_space=pltpu.SMEM),
            pl.BlockSpec((1, 1, 2 * TOP_K * half), lambda i: (jnp.minimum(i + 1, n_t - 1), 0, 0),
                         memory_space=pltpu.SMEM),
            pl.BlockSpec(memory_space=pl.ANY),
            pl.BlockSpec((tm, TOP_K), lambda i: (i, 0)),
            row, row,
            pl.BlockSpec((None, D, 2 * F), lambda i: (layer, 0, 0)),
            pl.BlockSpec((None, F, D), lambda i: (layer, 0, 0)),
            pl.BlockSpec((None, 1, D), lambda i: (i // per_b, 0, 0)),
            vec, vec,
        ],
        out_specs=row,
        out_shape=jax.ShapeDtypeStruct((T, D), F32),
        scratch_shapes=[pltpu.VMEM((2, TOP_K, half, D), F32), pltpu.SemaphoreType.DMA((2,))],
        compiler_params=_cparams(("arbitrary",)),
        name="combine",
    )(dest_steps, dest_steps, y_rows, gate, h2, x1, ws_gu, ws_d, g_f, ln_g.reshape(1, D), ln_b.reshape(1, D))


def _dispatch_tables(eid, pos, counts_rows, T):
    R = EXPERT_ROWS
    n_rows = T * TOP_K + N_EXPERTS * R
    n_blk = n_rows // R
    experts = jnp.arange(N_EXPERTS, dtype=jnp.int32)
    counts = counts_rows.astype(jnp.int32).reshape(GROUP_SIZE, N_GROUPS).T.reshape(N_EXPERTS)
    padded = (counts + R - 1) // R * R
    pad_ends = jnp.cumsum(padded)
    pad_starts = pad_ends - padded
    dest = pos + jnp.sum(jnp.where(eid[..., None] == experts, pad_starts, 0), axis=-1)
    blk_start = jnp.arange(n_blk, dtype=jnp.int32) * R
    blk_e = jnp.minimum(jnp.sum(pad_ends[None, :] <= blk_start[:, None], axis=1), N_EXPERTS - 1).astype(jnp.int32)
    seg_end = jnp.sum(jnp.where(blk_e[:, None] == experts, pad_starts + counts, 0), axis=1)
    blk_n = jnp.clip(seg_end - blk_start, 0, R).astype(jnp.int32)
    unused = pad_ends[-1] + experts * R
    tail = jnp.concatenate([jnp.where(padded > 0, pad_ends - R, -1),
                            jnp.where(unused < n_rows, unused, -1)]).astype(jnp.int32)
    return dest.astype(jnp.int32), blk_e, blk_n, tail, n_rows


def _tile_index(dest, tm):
    K, T = dest.shape
    return dest.reshape(K, T // tm, tm).transpose(1, 0, 2).reshape(T // tm, 1, K * tm)


def _trunk(x, mods, lams, p, depth):
    B, S, D = x.shape
    T = B * S
    aw = D // 2
    n_heads = aw // V_DIM
    alpha = (2 * depth) ** 0.25
    x = layer_norm_rows(x.reshape(T, D), p["emb_ln_g"], p["emb_ln_b"])
    for l in range(depth):
        mod = mods[l]
        sh_a, sc_a, g_a, sh_f, sc_f, g_f = (mod[:, i] for i in range(6))
        P = in_proj(x, sc_a, sh_a, p["w_in"], l, p["b_in"][l], S)
        lam_init = 0.8 - 0.6 * math.exp(-0.3 * l)
        attn = diff_attention(P, lams[l], p["attn_subln_g"][l], B, S, n_heads, 1.0 - lam_init)
        conv = conformer_conv(P, p["conv_w"][l], p["conv_b"][l], p["conv_ln_g"][l], p["conv_ln_b"][l],
                              B, S, 3 * aw // (D - aw), 3 * aw // (D - aw) + 1)
        x1, h2, logits_t = out_proj(attn, conv, x, p["w_out"], l, p["b_out"][l], g_a, p["ln1_g"][l], p["ln1_b"][l],
                                    sc_f, sh_f, p["w_router_t"][l], S, alpha)
        eid, gate, pos, counts_rows = route(logits_t, p["router_bias_rows"][l])
        dest, blk_e, blk_n, tail, n_rows = _dispatch_tables(eid, pos, counts_rows[:, 0], T)
        xs = dispatch(_tile_index(dest, _tile(S, DISPATCH_ROWS)), tail, h2, n_rows)
        y_rows = routed_experts(xs, blk_e, blk_n, p["w_gate_up"], p["w_down"], l)
        x = combine(_tile_index(dest, _tile(S, COMBINE_ROWS)), y_rows, gate.T, h2, x1, p["ws_gate_up"],
                    p["ws_down"], l, g_f, p["ln2_g"][l], p["ln2_b"][l], S, alpha)
    return x.reshape(B, S, D)


def kernel(x_prompt, x_sample, c_prompt, c_sample, emb_ln_g, emb_ln_b, w_ada, b_ada, w_in, b_in, lambda_q1, lambda_k1, lambda_q2, lambda_k2, attn_subln_g, conv_w, conv_b, conv_ln_g, conv_ln_b, w_out, b_out, ln1_g, ln1_b, w_router, router_bias, w_gate_up, w_down, ws_gate_up, ws_down, ln2_g, ln2_b):
    depth, D = w_in.shape[0], w_in.shape[1]
    Bp, Bs = x_prompt.shape[0], x_sample.shape[0]

    c_all = jnp.concatenate([c_prompt, c_sample], axis=0)
    pad = (-c_all.shape[0]) % 8
    c_all = jnp.pad(c_all, ((0, pad), (0, 0)))
    mod_all = ada_mod(c_all, w_ada, b_ada)
    mod_all = mod_all.reshape(depth, c_all.shape[0], 6, 1, D)
    lams = (jnp.exp(jnp.sum(lambda_q1 * lambda_k1, axis=-1)) - jnp.exp(jnp.sum(lambda_q2 * lambda_k2, axis=-1))
            + jnp.asarray([0.8 - 0.6 * math.exp(-0.3 * l) for l in range(depth)], F32))

    rows = jnp.arange(N_EXPERTS)
    perm = (rows % N_GROUPS) * GROUP_SIZE + rows // N_GROUPS
    params = dict(
        emb_ln_g=emb_ln_g, emb_ln_b=emb_ln_b,
        w_in=w_in.astype(BF16), b_in=b_in, attn_subln_g=attn_subln_g,
        conv_w=conv_w, conv_b=conv_b, conv_ln_g=conv_ln_g, conv_ln_b=conv_ln_b,
        w_out=w_out.astype(BF16), b_out=b_out, ln1_g=ln1_g, ln1_b=ln1_b,
        w_router_t=jnp.swapaxes(w_router, 1, 2)[:, perm, :],
        router_bias_rows=router_bias[:, perm].reshape(depth, N_EXPERTS, 1),
        w_gate_up=w_gate_up.astype(BF16), w_down=w_down.astype(BF16),
        ws_gate_up=ws_gate_up.astype(BF16), ws_down=ws_down.astype(BF16),
        ln2_g=ln2_g, ln2_b=ln2_b,
    )
    y_prompt = _trunk(x_prompt, mod_all[:, :Bp], lams, params, depth)
    y_sample = _trunk(x_sample, mod_all[:, Bp:Bp + Bs], lams, params, depth)
    return (y_prompt, y_sample)
```
